```python
import math
import jax, jax.numpy as jnp
from jax import lax
import numpy as np

D_MODEL = 1024
BATCH = 8
SEQ = 2048
DEPTH = 2

CHUNK = 64
N_MIXERS = 2
SB_HEADS = 16
SB_HEAD_DIM = D_MODEL // SB_HEADS
Q_BLOCK = 128
LRU_WIDTH = D_MODEL
LRU_BLOCKS = 16
LRU_BLOCK_W = LRU_WIDTH // LRU_BLOCKS
LRU_C = 8.0
CONV_W = 4
D_FF = 2816
NORM_EPS = 1e-6

kernel_name = "macaron_stickbreak_rglru_hybrid"


def rms_norm(x, gain):
    xf = x.astype(jnp.float32)
    y = xf * lax.rsqrt(jnp.mean(xf * xf, axis=-1, keepdims=True) + NORM_EPS)
    return (y * gain.astype(jnp.float32)).astype(x.dtype)


def swiglu(xn, w_in, w_out):
    gate, up = jnp.split(xn @ w_in, 2, axis=-1)
    return (jax.nn.silu(gate) * up) @ w_out


def stick_breaking_attention(xn, w_qkv, q_gain, k_gain, w_o):
    b, s, _ = xn.shape
    qkv = (xn @ w_qkv).reshape(b, s, 3, SB_HEADS, SB_HEAD_DIM)
    q = rms_norm(qkv[:, :, 0], q_gain).astype(jnp.float32).transpose(0, 2, 1, 3)
    k = rms_norm(qkv[:, :, 1], k_gain).astype(jnp.float32).transpose(0, 2, 1, 3)
    v = qkv[:, :, 2].astype(jnp.float32).transpose(0, 2, 1, 3)
    scale = 1.0 / math.sqrt(SB_HEAD_DIM)
    outs = []
    for blk in range(s // Q_BLOCK):
        lo = blk * Q_BLOCK
        hi = lo + Q_BLOCK
        z = jnp.einsum('bhqd,bhkd->bhqk', q[:, :, lo:hi], k[:, :, :hi]) * scale
        t_idx = lo + jnp.arange(Q_BLOCK)[:, None]
        s_idx = jnp.arange(hi)[None, :]
        mask = s_idx < t_idx
        log_beta = jax.nn.log_sigmoid(z)
        log_1m = jnp.where(mask, jax.nn.log_sigmoid(-z), 0.0)
        rev = lax.cumsum(log_1m, axis=3, reverse=True)
        rev_excl = jnp.concatenate([rev[..., 1:], jnp.zeros_like(rev[..., :1])], axis=-1)
        w = jnp.where(mask, jnp.exp(log_beta + rev_excl), 0.0)
        outs.append(jnp.einsum('bhqk,bhkd->bhqd', w, v[:, :, :hi]))
    o = jnp.concatenate(outs, axis=2)
    o = o.transpose(0, 2, 1, 3).reshape(b, s, SB_HEADS * SB_HEAD_DIM).astype(xn.dtype)
    return o @ w_o


def rglru_block(xn, w_in, conv_w, conv_b, w_r, b_r, w_i, b_i, lam, w_o):
    b, s, _ = xn.shape
    xb, yb = jnp.split(xn @ w_in, 2, axis=-1)
    y = jax.nn.gelu(yb, approximate=True)
    xpad = jnp.pad(xb, ((0, 0), (CONV_W - 1, 0), (0, 0)))
    xc = conv_b + sum(conv_w[j] * xpad[:, j:j + s] for j in range(CONV_W))
    xh = xc.reshape(b, s, LRU_BLOCKS, LRU_BLOCK_W)
    r = jax.nn.sigmoid(jnp.einsum('bsnc,ncd->bsnd', xh, w_r).reshape(b, s, LRU_WIDTH) + b_r)
    i = jax.nn.sigmoid(jnp.einsum('bsnc,ncd->bsnd', xh, w_i).reshape(b, s, LRU_WIDTH) + b_i)
    log_a = LRU_C * r.astype(jnp.float32) * jax.nn.log_sigmoid(lam.astype(jnp.float32))
    a = jnp.exp(log_a)
    mult = jnp.sqrt(-jnp.expm1(2.0 * log_a))
    u = mult * (i * xc).astype(jnp.float32)

    def combine(left, right):
        a_l, h_l = left
        a_r, h_r = right
        return a_l * a_r, a_r * h_l + h_r

    _, h = lax.associative_scan(combine, (a, u), axis=1)
    return (h.astype(xn.dtype) * y) @ w_o


def setup_inputs(seed: int = 0) -> dict:
    key = jax.random.key(seed)
    keys = iter(jax.random.split(key, 40))

    def normal(shape, scale):
        return jax.random.normal(next(keys), shape, jnp.float32) * scale

    def gain(n):
        return 1.0 + normal((n,), 0.02)

    def lam_init(n):
        a_c = jax.random.uniform(next(keys), (n,), jnp.float32, 0.9, 0.999)
        a = a_c ** (1.0 / LRU_C)
        return jnp.log(a) - jnp.log1p(-a)

    d, f, w = D_MODEL, D_FF, LRU_WIDTH
    p = {}
    p["x"] = normal((BATCH, SEQ, d), 1.0)
    p["l0_ff1_norm"] = gain(d)
    p["l0_ff1_w_in"] = normal((d, 2 * f), d ** -0.5)
    p["l0_ff1_w_out"] = normal((f, d), f ** -0.5)
    p["l0_mix_norm"] = gain(d)
    p["l0_sb_w_qkv"] = normal((d, 3 * SB_HEADS * SB_HEAD_DIM), d ** -0.5)
    p["l0_sb_q_norm"] = gain(SB_HEAD_DIM)
    p["l0_sb_k_norm"] = gain(SB_HEAD_DIM)
    p["l0_sb_w_o"] = normal((SB_HEADS * SB_HEAD_DIM, d), d ** -0.5)
    p["l0_ff2_norm"] = gain(d)
    p["l0_ff2_w_in"] = normal((d, 2 * f), d ** -0.5)
    p["l0_ff2_w_out"] = normal((f, d), f ** -0.5)
    p["l1_ff1_norm"] = gain(d)
    p["l1_ff1_w_in"] = normal((d, 2 * f), d ** -0.5)
    p["l1_ff1_w_out"] = normal((f, d), f ** -0.5)
    p["l1_mix_norm"] = gain(d)
    p["l1_lru_w_in"] = normal((d, 2 * w), d ** -0.5)
    p["l1_lru_conv_w"] = normal((CONV_W, w), CONV_W ** -0.5)
    p["l1_lru_conv_b"] = normal((w,), 0.01)
    p["l1_lru_w_r"] = normal((LRU_BLOCKS, LRU_BLOCK_W, LRU_BLOCK_W), LRU_BLOCK_W ** -0.5)
    p["l1_lru_b_r"] = normal((w,), 0.01)
    p["l1_lru_w_i"] = normal((LRU_BLOCKS, LRU_BLOCK_W, LRU_BLOCK_W), LRU_BLOCK_W ** -0.5)
    p["l1_lru_b_i"] = normal((w,), 0.01)
    p["l1_lru_lambda"] = lam_init(w)
    p["l1_lru_w_o"] = normal((w, d), w ** -0.5)
    p["l1_ff2_norm"] = gain(d)
    p["l1_ff2_w_in"] = normal((d, 2 * f), d ** -0.5)
    p["l1_ff2_w_out"] = normal((f, d), f ** -0.5)
    return p


def reference(x,
              l0_ff1_norm, l0_ff1_w_in, l0_ff1_w_out,
              l0_mix_norm, l0_sb_w_qkv, l0_sb_q_norm, l0_sb_k_norm, l0_sb_w_o,
              l0_ff2_norm, l0_ff2_w_in, l0_ff2_w_out,
              l1_ff1_norm, l1_ff1_w_in, l1_ff1_w_out,
              l1_mix_norm, l1_lru_w_in, l1_lru_conv_w, l1_lru_conv_b,
              l1_lru_w_r, l1_lru_b_r, l1_lru_w_i, l1_lru_b_i, l1_lru_lambda, l1_lru_w_o,
              l1_ff2_norm, l1_ff2_w_in, l1_ff2_w_out):
    ffn1 = [(l0_ff1_norm, l0_ff1_w_in, l0_ff1_w_out),
            (l1_ff1_norm, l1_ff1_w_in, l1_ff1_w_out)]
    ffn2 = [(l0_ff2_norm, l0_ff2_w_in, l0_ff2_w_out),
            (l1_ff2_norm, l1_ff2_w_in, l1_ff2_w_out)]
    mixers = [(l0_mix_norm, (l0_sb_w_qkv, l0_sb_q_norm, l0_sb_k_norm, l0_sb_w_o)),
              (l1_mix_norm, (l1_lru_w_in, l1_lru_conv_w, l1_lru_conv_b, l1_lru_w_r, l1_lru_b_r,
                             l1_lru_w_i, l1_lru_b_i, l1_lru_lambda, l1_lru_w_o))]
    for layer in range(DEPTH):
        n1, wi1, wo1 = ffn1[layer]
        x = x + 0.5 * swiglu(rms_norm(x, n1), wi1, wo1)
        mix_norm, mix_params = mixers[layer]
        xn = rms_norm(x, mix_norm)
        if layer % N_MIXERS == 0:
            x = x + stick_breaking_attention(xn, *mix_params)
        else:
            x = x + rglru_block(xn, *mix_params)
        n2, wi2, wo2 = ffn2[layer]
        x = x + 0.5 * swiglu(rms_norm(x, n2), wi2, wo2)
    return x
```

```python
import functools
import math

import jax
import jax.numpy as jnp
from jax import lax
from jax.experimental import pallas as pl
from jax.experimental.pallas import tpu as pltpu

D_MODEL = 1024
SB_HEADS = 16
SB_HEAD_DIM = D_MODEL // SB_HEADS
LRU_BLOCKS = 16
LRU_BLOCK_W = D_MODEL // LRU_BLOCKS
LRU_C = 8.0
CONV_W = 4
NORM_EPS = 1e-6

V7X_LANES = 128
V7X_SUBLANES = 8
V7X_MXU_DIM = 256
V7X_VMEM_BYTES = 64 * 1024 * 1024

F32 = jnp.float32
BF16 = jnp.bfloat16

FFN_ROWS = 512
FFN_CHUNK = V7X_MXU_DIM
PROJ_ROWS = 512
SB_BLOCK = 128
LRU_STEPS = 64


def _vmem_limit(resident_bytes):
    return int(min(resident_bytes * 3 // 2 + (4 << 20), V7X_VMEM_BYTES - (4 << 20)))


def _dot(a, b):
    return jnp.dot(a, b, preferred_element_type=F32)


def _rms_norm_rows(x, gain_row):
    ms = jnp.mean(x * x, axis=-1, keepdims=True)
    return x * lax.rsqrt(ms + NORM_EPS) * gain_row


def _split_bf16(v):
    hi = v.astype(BF16)
    lo = (v - hi.astype(F32)).astype(BF16)
    return hi, lo


def _resident(shape):
    zeros = (0,) * len(shape)
    return pl.BlockSpec(shape, lambda *_: zeros, pipeline_mode=pl.Buffered(1))


def _ffn_kernel(x_ref, g_ref, win_ref, wout_ref, o_ref, *, d_ff):
    x = x_ref[...]
    xn = _rms_norm_rows(x, g_ref[...]).astype(BF16)
    y = jnp.zeros_like(x)
    for c in range(d_ff // FFN_CHUNK):
        lo = c * FFN_CHUNK
        gate = _dot(xn, win_ref[:, lo:lo + FFN_CHUNK])
        up = _dot(xn, win_ref[:, d_ff + lo:d_ff + lo + FFN_CHUNK])
        h = (gate * jax.nn.sigmoid(gate) * up).astype(BF16)
        y = y + _dot(h, wout_ref[lo:lo + FFN_CHUNK, :])
    o_ref[...] = x + 0.5 * y


def _row_spec(rows, time_major):
    if time_major:
        return pl.BlockSpec((rows, D_MODEL), lambda b, t: (t, b))
    return pl.BlockSpec((None, rows, D_MODEL), lambda b, t: (b, t, 0))


def _ffn(x, gain, w_in, w_out, *, batch, seq, in_time_major, out_time_major):
    d_ff = w_out.shape[0]
    assert d_ff % FFN_CHUNK == 0 and seq % FFN_ROWS == 0
    out_shape = (seq, batch * D_MODEL) if out_time_major else (batch, seq, D_MODEL)
    resident = 2 * (w_in.size + w_out.size) + 4 * 4 * FFN_ROWS * D_MODEL \
        + 4 * FFN_ROWS * (3 * D_MODEL + 3 * FFN_CHUNK)
    return pl.pallas_call(
        functools.partial(_ffn_kernel, d_ff=d_ff),
        grid=(batch, seq // FFN_ROWS),
        in_specs=[_row_spec(FFN_ROWS, in_time_major),
                  _resident((1, D_MODEL)),
                  _resident(w_in.shape),
                  _resident(w_out.shape)],
        out_specs=_row_spec(FFN_ROWS, out_time_major),
        out_shape=jax.ShapeDtypeStruct(out_shape, F32),
        compiler_params=pltpu.CompilerParams(
            dimension_semantics=("arbitrary", "arbitrary"),
            vmem_limit_bytes=_vmem_limit(resident)),
        name="ffn",
    )(x, gain.reshape(1, D_MODEL), w_in, w_out)


def _head_rms_norm(t, mean_ref, gain_row):
    outs = []
    for c in range(D_MODEL // V7X_MXU_DIM):
        tc = t[:, c * V7X_MXU_DIM:(c + 1) * V7X_MXU_DIM]
        hi, lo = _split_bf16(tc * tc)
        ms = _dot(hi, mean_ref[...]) + _dot(lo, mean_ref[...])
        outs.append(tc * lax.rsqrt(ms + NORM_EPS))
    return jnp.concatenate(outs, axis=-1) * gain_row


def _qkv_kernel(x_ref, g_ref, w_ref, mean_ref, qg_ref, kg_ref, q_ref, k_ref, v_ref):
    xn = _rms_norm_rows(x_ref[...], g_ref[...]).astype(BF16)
    q = _dot(xn, w_ref[:, 0:D_MODEL])
    q_ref[...] = (_head_rms_norm(q, mean_ref, qg_ref[...]) * (1.0 / math.sqrt(SB_HEAD_DIM))).astype(BF16)
    k = _dot(xn, w_ref[:, D_MODEL:2 * D_MODEL])
    k_ref[...] = _head_rms_norm(k, mean_ref, kg_ref[...]).astype(BF16)
    v_ref[...] = _dot(xn, w_ref[:, 2 * D_MODEL:3 * D_MODEL]).astype(BF16)


def _head_mean_matrix():
    r = lax.broadcasted_iota(jnp.int32, (V7X_MXU_DIM, V7X_MXU_DIM), 0) // SB_HEAD_DIM
    c = lax.broadcasted_iota(jnp.int32, (V7X_MXU_DIM, V7X_MXU_DIM), 1) // SB_HEAD_DIM
    return jnp.where(r == c, 1.0 / SB_HEAD_DIM, 0.0).astype(BF16)


def _qkv(x, gain, w_qkv, q_gain, k_gain):
    n = x.shape[0]
    row = pl.BlockSpec((PROJ_ROWS, D_MODEL), lambda i: (i, 0))
    out = jax.ShapeDtypeStruct((n, D_MODEL), BF16)
    resident = 2 * w_qkv.size + 4 * 2 * PROJ_ROWS * D_MODEL + 3 * 2 * 2 * PROJ_ROWS * D_MODEL \
        + 4 * 4 * PROJ_ROWS * D_MODEL
    tile_gain = lambda g: jnp.tile(g, SB_HEADS).reshape(1, D_MODEL)
    return pl.pallas_call(
        _qkv_kernel,
        grid=(n // PROJ_ROWS,),
        in_specs=[row, _resident((1, D_MODEL)), _resident(w_qkv.shape),
                  _resident((V7X_MXU_DIM, V7X_MXU_DIM)),
                  _resident((1, D_MODEL)), _resident((1, D_MODEL))],
        out_specs=[row, row, row],
        out_shape=[out, out, out],
        compiler_params=pltpu.CompilerParams(
            dimension_semantics=("arbitrary",), vmem_limit_bytes=_vmem_limit(resident)),
        name="qkv",
    )(x, gain.reshape(1, D_MODEL), w_qkv, _head_mean_matrix(), tile_gain(q_gain), tile_gain(k_gain))


def _sb_kernel(q_ref, k_ref, v_ref, tri_ref, ones_ref, o_ref, *, seq):
    T = SB_BLOCK
    lane = lax.broadcasted_iota(jnp.int32, (T, V7X_LANES), 1)
    head0_lanes = jnp.where(lane < SB_HEAD_DIM, 1.0, 0.0).astype(BF16)
    head1_lanes = jnp.where(lane < SB_HEAD_DIM, 0.0, 1.0).astype(BF16)
    row = lax.broadcasted_iota(jnp.int32, (T, 2 * T), 0)
    col = lax.broadcasted_iota(jnp.int32, (T, 2 * T), 1)
    strictly_causal = (col & (T - 1)) < row

    def per_head_rows(ref, kb):
        blk = ref[pl.ds(pl.multiple_of(kb * T, T), T), :]
        return jnp.concatenate([blk * head0_lanes, blk * head1_lanes], axis=0)

    def block(q, kb, carry, acc, diagonal):
        z = lax.dot_general(q, per_head_rows(k_ref, kb), (((1,), (1,)), ((), ())),
                            preferred_element_type=F32)
        s = jnp.maximum(z, 0.0) + jnp.log(1.0 + jnp.exp(-jnp.abs(z)))
        if diagonal:
            s = jnp.where(strictly_causal, s, 0.0)
        s_hi, s_lo = _split_bf16(s)
        s_parts = jnp.concatenate([s_hi, s_lo], axis=0)
        later = _dot(s_parts, tri_ref[...])
        total = _dot(s_parts, ones_ref[...])
        later = later[:T] + later[T:]
        total = total[:T] + total[T:]
        w = jnp.exp((z - s) - later - carry)
        if diagonal:
            w = jnp.where(strictly_causal, w, 0.0)
        acc = acc + _dot(w.astype(BF16), per_head_rows(v_ref, kb))
        return carry + total, acc

    def q_body(qi, _):
        rows = pl.ds(pl.multiple_of(qi * T, T), T)
        q = q_ref[rows, :]
        state = block(q, qi, jnp.zeros((T, 2 * T), F32), jnp.zeros((T, V7X_LANES), F32), True)
        carry, acc = lax.fori_loop(
            0, qi, lambda j, st: block(q, qi - 1 - j, st[0], st[1], False), state)
        o_ref[rows, :] = acc.astype(o_ref.dtype)
        return 0

    lax.fori_loop(0, seq // T, q_body, 0)


def _sb_constants():
    n = 2 * SB_BLOCK
    r = lax.broadcasted_iota(jnp.int32, (n, n), 0)
    c = lax.broadcasted_iota(jnp.int32, (n, n), 1)
    same = (r // SB_BLOCK) == (c // SB_BLOCK)
    tri = jnp.where(same & (r > c), 1.0, 0.0).astype(BF16)
    ones = jnp.where(same, 1.0, 0.0).astype(BF16)
    return tri, ones


def _stick_breaking(q, k, v, *, batch, seq):
    assert seq % SB_BLOCK == 0
    tri, ones = _sb_constants()
    group = pl.BlockSpec((None, seq, V7X_LANES), lambda b, g: (b, 0, g))
    const = _resident((2 * SB_BLOCK, 2 * SB_BLOCK))
    resident = 2 * 4 * 2 * seq * V7X_LANES + 2 * 2 * tri.size + 4 * 16 * SB_BLOCK * 2 * SB_BLOCK
    return pl.pallas_call(
        functools.partial(_sb_kernel, seq=seq),
        grid=(batch, D_MODEL // V7X_LANES),
        in_specs=[group, group, group, const, const],
        out_specs=group,
        out_shape=jax.ShapeDtypeStruct((batch, seq, D_MODEL), BF16),
        compiler_params=pltpu.CompilerParams(
            dimension_semantics=("arbitrary", "arbitrary"),
            vmem_limit_bytes=_vmem_limit(resident)),
        name="stick_breaking",
    )(q, k, v, tri, ones)


def _proj_residual_kernel(a_ref, w_ref, x_ref, o_ref):
    o_ref[...] = x_ref[...] + _dot(a_ref[...], w_ref[...])


def _proj_residual(a, w, x):
    n = x.shape[0]
    row = pl.BlockSpec((PROJ_ROWS, D_MODEL), lambda i: (i, 0))
    resident = 2 * w.size + 2 * PROJ_ROWS * D_MODEL * (2 + 4 + 4) + 4 * PROJ_ROWS * D_MODEL
    return pl.pallas_call(
        _proj_residual_kernel,
        grid=(n // PROJ_ROWS,),
        in_specs=[row, _resident(w.shape), row],
        out_specs=row,
        out_shape=jax.ShapeDtypeStruct((n, D_MODEL), F32),
        compiler_params=pltpu.CompilerParams(
            dimension_semantics=("arbitrary",), vmem_limit_bytes=_vmem_limit(resident)),
        name="proj_residual",
    )(a, w, x)


def _rglru_kernel(x_ref, g_ref, win_ref, cw_ref, cb_ref, wg_ref, br_ref, bi_ref, lam_ref, wo_ref,
                  o_ref, xb_s, y_s, a_s, u_s, h_s, state_s, *, batch):
    rows = LRU_STEPS * batch
    halo = CONV_W * batch

    @pl.when(pl.program_id(0) == 0)
    def _():
        xb_s[0:halo, :] = jnp.zeros((halo, D_MODEL), F32)
        state_s[...] = jnp.zeros_like(state_s)

    x = x_ref[...]
    xn = _rms_norm_rows(x, g_ref[...]).astype(BF16)
    xb_s[halo:halo + rows, :] = _dot(xn, win_ref[:, 0:D_MODEL])
    y_s[...] = jax.nn.gelu(_dot(xn, win_ref[:, D_MODEL:2 * D_MODEL]), approximate=True)

    xc = cb_ref[...] + cw_ref[CONV_W - 1:CONV_W, :] * xb_s[halo:halo + rows, :]
    for j in range(CONV_W - 1):
        off = halo - (CONV_W - 1 - j) * batch
        xc = xc + cw_ref[j:j + 1, :] * xb_s[off:off + rows, :]
    xb_s[0:halo, :] = xb_s[rows:rows + halo, :]

    xc_b = xc.astype(BF16)
    r_parts, i_parts = [], []
    for g in range(D_MODEL // V7X_MXU_DIM):
        ri = _dot(xc_b[:, g * V7X_MXU_DIM:(g + 1) * V7X_MXU_DIM], wg_ref[g])
        r_parts.append(ri[:, :V7X_MXU_DIM])
        i_parts.append(ri[:, V7X_MXU_DIM:])
    r = jax.nn.sigmoid(jnp.concatenate(r_parts, axis=-1) + br_ref[...])
    i = jax.nn.sigmoid(jnp.concatenate(i_parts, axis=-1) + bi_ref[...])
    lam = lam_ref[...]
    log_sig_lam = jnp.minimum(lam, 0.0) - jnp.log1p(jnp.exp(-jnp.abs(lam)))
    log_a = (LRU_C * r) * log_sig_lam
    a_s[...] = jnp.exp(log_a)
    th = jnp.tanh(log_a)
    u_s[...] = jnp.sqrt(-2.0 * th / (1.0 - th)) * (i * xc)

    def step(t, h):
        sl = pl.ds(pl.multiple_of(t * batch, batch), batch)
        h = a_s[sl, :] * h + u_s[sl, :]
        h_s[sl, :] = h
        return h

    state_s[...] = lax.fori_loop(0, LRU_STEPS, step, state_s[...], unroll=8)
    o_ref[...] = x + _dot((h_s[...] * y_s[...]).astype(BF16), wo_ref[...])


def _block_diag_tiles(w):
    per = V7X_MXU_DIM // LRU_BLOCK_W
    w = w.reshape(LRU_BLOCKS // per, per, LRU_BLOCK_W, LRU_BLOCK_W)
    eye = jnp.eye(per, dtype=w.dtype)
    return jnp.einsum("gpcd,pq->gpcqd", w, eye).reshape(LRU_BLOCKS // per, V7X_MXU_DIM, V7X_MXU_DIM)


def _rglru(x, gain, w_in, conv_w, conv_b, w_r, b_r, w_i, b_i, lam, w_o, *, batch):
    assert batch == V7X_SUBLANES
    n = x.shape[0]
    rows = LRU_STEPS * batch
    w_gate = jnp.concatenate([_block_diag_tiles(w_r), _block_diag_tiles(w_i)], axis=-1).astype(BF16)
    vec = lambda p: p.reshape(1, D_MODEL)
    row = pl.BlockSpec((rows, D_MODEL), lambda i: (i, 0))
    resident = 2 * (w_in.size + w_gate.size + w_o.size) + 4 * 4 * rows * D_MODEL \
        + 4 * 5 * (rows + CONV_W * batch) * D_MODEL + 4 * 6 * rows * D_MODEL
    return pl.pallas_call(
        functools.partial(_rglru_kernel, batch=batch),
        grid=(n // rows,),
        in_specs=[row, _resident((1, D_MODEL)), _resident(w_in.shape),
                  _resident((CONV_W, D_MODEL)), _resident((1, D_MODEL)),
                  _resident(w_gate.shape), _resident((1, D_MODEL)), _resident((1, D_MODEL)),
                  _resident((1, D_MODEL)), _resident(w_o.shape)],
        out_specs=row,
        out_shape=jax.ShapeDtypeStruct((n, D_MODEL), F32),
        scratch_shapes=[pltpu.VMEM((rows + CONV_W * batch, D_MODEL), F32),
                        pltpu.VMEM((rows, D_MODEL), F32),
                        pltpu.VMEM((rows, D_MODEL), F32),
                        pltpu.VMEM((rows, D_MODEL), F32),
                        pltpu.VMEM((rows, D_MODEL), F32),
                        pltpu.VMEM((batch, D_MODEL), F32)],
        compiler_params=pltpu.CompilerParams(
            dimension_semantics=("arbitrary",), vmem_limit_bytes=_vmem_limit(resident)),
        name="rglru",
    )(x, vec(gain), w_in, conv_w, vec(conv_b), w_gate, vec(b_r), vec(b_i), vec(lam), w_o)


def kernel(x, l0_ff1_norm, l0_ff1_w_in, l0_ff1_w_out, l0_mix_norm, l0_sb_w_qkv, l0_sb_q_norm, l0_sb_k_norm, l0_sb_w_o, l0_ff2_norm, l0_ff2_w_in, l0_ff2_w_out, l1_ff1_norm, l1_ff1_w_in, l1_ff1_w_out, l1_mix_norm, l1_lru_w_in, l1_lru_conv_w, l1_lru_conv_b, l1_lru_w_r, l1_lru_b_r, l1_lru_w_i, l1_lru_b_i, l1_lru_lambda, l1_lru_w_o, l1_ff2_norm, l1_ff2_w_in, l1_ff2_w_out):
    batch, seq, d = x.shape
    assert d == D_MODEL
    n = batch * seq
    bf = lambda w: w.astype(BF16)
    ffn = functools.partial(_ffn, batch=batch, seq=seq)

    x = ffn(x, l0_ff1_norm, bf(l0_ff1_w_in), bf(l0_ff1_w_out), in_time_major=False, out_time_major=False)
    x2 = x.reshape(n, D_MODEL)
    q, k, v = _qkv(x2, l0_mix_norm, bf(l0_sb_w_qkv), l0_sb_q_norm, l0_sb_k_norm)
    as_seq = lambda t: t.reshape(batch, seq, D_MODEL)
    o = _stick_breaking(as_seq(q), as_seq(k), as_seq(v), batch=batch, seq=seq)
    x2 = _proj_residual(o.reshape(n, D_MODEL), bf(l0_sb_w_o), x2)
    x = ffn(as_seq(x2), l0_ff2_norm, bf(l0_ff2_w_in), bf(l0_ff2_w_out), in_time_major=False, out_time_major=False)

    x = ffn(x, l1_ff1_norm, bf(l1_ff1_w_in), bf(l1_ff1_w_out), in_time_major=False, out_time_major=True)
    x2 = _rglru(x.reshape(n, D_MODEL), l1_mix_norm, bf(l1_lru_w_in), l1_lru_conv_w, l1_lru_conv_b,
                l1_lru_w_r, l1_lru_b_r, l1_lru_w_i, l1_lru_b_i, l1_lru_lambda, bf(l1_lru_w_o), batch=batch)
    x = ffn(x2.reshape(seq, batch * D_MODEL), l1_ff2_norm, bf(l1_ff2_w_in), bf(l1_ff2_w_out),
            in_time_major=True, out_time_major=False)
    return x
```

```python
import functools
import math

import jax
import jax.numpy as jnp
from jax import lax
from jax.experimental import pallas as pl
from jax.experimental.pallas import tpu as pltpu

D_MODEL = 1024
SB_HEADS = 16
SB_HEAD_DIM = D_MODEL // SB_HEADS
LRU_BLOCKS = 16
LRU_BLOCK_W = D_MODEL // LRU_BLOCKS
LRU_C = 8.0
CONV_W = 4
NORM_EPS = 1e-6
LOG2_E = math.log2(math.e)

V7X_LANES = 128
V7X_SUBLANES = 8
V7X_MXU_DIM = 256
V7X_VMEM_BYTES = 64 * 1024 * 1024

F32 = jnp.float32
BF16 = jnp.bfloat16

FFN_ROWS = 512
FFN_CHUNK = V7X_MXU_DIM
PROJ_ROWS = 512
SB_QUERIES = 256
SB_KEYS = 128
SB_UNROLL = 4
SB_GROUPS = 2
LRU_STEPS = 64


def _vmem_limit(resident_bytes):
    return int(min(resident_bytes * 3 // 2 + (4 << 20), V7X_VMEM_BYTES - (4 << 20)))


def _dot(a, b):
    return jnp.dot(a, b, preferred_element_type=F32)


def _rms_norm_rows(x, gain_row):
    ms = jnp.mean(x * x, axis=-1, keepdims=True)
    return x * lax.rsqrt(ms + NORM_EPS) * gain_row


def _split_bf16(v):
    hi = v.astype(BF16)
    lo = (v - hi.astype(F32)).astype(BF16)
    return hi, lo


def _resident(shape):
    zeros = (0,) * len(shape)
    return pl.BlockSpec(shape, lambda *_: zeros, pipeline_mode=pl.Buffered(1))


def _ffn_kernel(x_ref, g_ref, win_ref, wout_ref, o_ref, *, d_ff):
    x = x_ref[...]
    xn = _rms_norm_rows(x, g_ref[...]).astype(BF16)
    y = jnp.zeros_like(x)
    for c in range(d_ff // FFN_CHUNK):
        lo = c * FFN_CHUNK
        gate = _dot(xn, win_ref[:, lo:lo + FFN_CHUNK])
        up = _dot(xn, win_ref[:, d_ff + lo:d_ff + lo + FFN_CHUNK])
        h = (gate * jax.nn.sigmoid(gate) * up).astype(BF16)
        y = y + _dot(h, wout_ref[lo:lo + FFN_CHUNK, :])
    o_ref[...] = x + 0.5 * y


def _row_spec(rows, time_major):
    if time_major:
        return pl.BlockSpec((rows, D_MODEL), lambda b, t: (t, b))
    return pl.BlockSpec((None, rows, D_MODEL), lambda b, t: (b, t, 0))


def _ffn(x, gain, w_in, w_out, *, batch, seq, in_time_major, out_time_major):
    d_ff = w_out.shape[0]
    assert d_ff % FFN_CHUNK == 0 and seq % FFN_ROWS == 0
    out_shape = (seq, batch * D_MODEL) if out_time_major else (batch, seq, D_MODEL)
    resident = 2 * (w_in.size + w_out.size) + 4 * 4 * FFN_ROWS * D_MODEL \
        + 4 * FFN_ROWS * (3 * D_MODEL + 3 * FFN_CHUNK)
    return pl.pallas_call(
        functools.partial(_ffn_kernel, d_ff=d_ff),
        grid=(batch, seq // FFN_ROWS),
        in_specs=[_row_spec(FFN_ROWS, in_time_major),
                  _resident((1, D_MODEL)),
                  _resident(w_in.shape),
                  _resident(w_out.shape)],
        out_specs=_row_spec(FFN_ROWS, out_time_major),
        out_shape=jax.ShapeDtypeStruct(out_shape, F32),
        compiler_params=pltpu.CompilerParams(
            dimension_semantics=("arbitrary", "arbitrary"),
            vmem_limit_bytes=_vmem_limit(resident)),
        name="ffn",
    )(x, gain.reshape(1, D_MODEL), w_in, w_out)


def _head_rms_norm(t, mean_ref, gain_row):
    outs = []
    for c in range(D_MODEL // V7X_MXU_DIM):
        tc = t[:, c * V7X_MXU_DIM:(c + 1) * V7X_MXU_DIM]
        hi, lo = _split_bf16(tc * tc)
        ms = _dot(hi, mean_ref[...]) + _dot(lo, mean_ref[...])
        outs.append(tc * lax.rsqrt(ms + NORM_EPS))
    return jnp.concatenate(outs, axis=-1) * gain_row


def _qkv_kernel(x_ref, g_ref, w_ref, mean_ref, qg_ref, kg_ref, q_ref, k_ref, v_ref):
    xn = _rms_norm_rows(x_ref[...], g_ref[...]).astype(BF16)
    q = _dot(xn, w_ref[:, 0:D_MODEL])
    q_ref[...] = (_head_rms_norm(q, mean_ref, qg_ref[...]) * (1.0 / math.sqrt(SB_HEAD_DIM))).astype(BF16)
    k = _dot(xn, w_ref[:, D_MODEL:2 * D_MODEL])
    k_ref[...] = _head_rms_norm(k, mean_ref, kg_ref[...]).astype(BF16)
    v_ref[...] = _dot(xn, w_ref[:, 2 * D_MODEL:3 * D_MODEL]).astype(BF16)


def _head_mean_matrix():
    r = lax.broadcasted_iota(jnp.int32, (V7X_MXU_DIM, V7X_MXU_DIM), 0) // SB_HEAD_DIM
    c = lax.broadcasted_iota(jnp.int32, (V7X_MXU_DIM, V7X_MXU_DIM), 1) // SB_HEAD_DIM
    return jnp.where(r == c, 1.0 / SB_HEAD_DIM, 0.0).astype(BF16)


def _qkv(x, gain, w_qkv, q_gain, k_gain):
    n = x.shape[0]
    row = pl.BlockSpec((PROJ_ROWS, D_MODEL), lambda i: (i, 0))
    out = jax.ShapeDtypeStruct((n, D_MODEL), BF16)
    resident = 2 * w_qkv.size + 4 * 2 * PROJ_ROWS * D_MODEL + 3 * 2 * 2 * PROJ_ROWS * D_MODEL \
        + 4 * 4 * PROJ_ROWS * D_MODEL
    tile_gain = lambda g: jnp.tile(g, SB_HEADS).reshape(1, D_MODEL)
    return pl.pallas_call(
        _qkv_kernel,
        grid=(n // PROJ_ROWS,),
        in_specs=[row, _resident((1, D_MODEL)), _resident(w_qkv.shape),
                  _resident((V7X_MXU_DIM, V7X_MXU_DIM)),
                  _resident((1, D_MODEL)), _resident((1, D_MODEL))],
        out_specs=[row, row, row],
        out_shape=[out, out, out],
        compiler_params=pltpu.CompilerParams(
            dimension_semantics=("arbitrary",), vmem_limit_bytes=_vmem_limit(resident)),
        name="qkv",
    )(x, gain.reshape(1, D_MODEL), w_qkv, _head_mean_matrix(), tile_gain(q_gain), tile_gain(k_gain))


def _sb_kernel(q_ref, k_ref, v_ref, cs_ref, o_ref, *, seq):
    Q, K = SB_QUERIES, SB_KEYS
    lane = lax.broadcasted_iota(jnp.int32, (K, V7X_LANES), 1)
    head0_lanes = jnp.where(lane < SB_HEAD_DIM, 1.0, 0.0).astype(BF16)
    head1_lanes = jnp.where(lane < SB_HEAD_DIM, 0.0, 1.0).astype(BF16)
    lanes = [slice(g * V7X_LANES, (g + 1) * V7X_LANES) for g in range(SB_GROUPS)]

    def per_head_rows(ref, g, kb):
        blk = ref[pl.ds(pl.multiple_of(kb * K, K), K), lanes[g]]
        return jnp.concatenate([blk * head0_lanes, blk * head1_lanes], axis=0)

    def sweep(qs, q0, kbs, state, diagonal):
        work = [(g, kb) for g in range(SB_GROUPS) for kb in kbs]
        logits, s_parts, masks = [], [], []
        for g, kb in work:
            z = lax.dot_general(qs[g], per_head_rows(k_ref, g, kb), (((1,), (1,)), ((), ())),
                                preferred_element_type=F32)
            s = jnp.maximum(z, 0.0) + jnp.log(1.0 + jnp.exp2(jnp.abs(z) * -LOG2_E))
            if diagonal:
                row = lax.broadcasted_iota(jnp.int32, (Q, 2 * K), 0)
                col = lax.broadcasted_iota(jnp.int32, (Q, 2 * K), 1)
                mask = (kb * K + (col & (K - 1))) < (q0 + row)
                s = jnp.where(mask, s, 0.0)
                masks.append(mask)
            s_hi, s_lo = _split_bf16(s)
            for h in range(2):
                s_parts.append(jnp.concatenate([s_hi[:, h * K:(h + 1) * K],
                                                s_lo[:, h * K:(h + 1) * K]], axis=1))
            logits.append(z)
        sums = _dot(jnp.concatenate(s_parts, axis=0), cs_ref[...])
        state = list(state)
        for n, (g, kb) in enumerate(work):
            carry, acc = state[g]
            h0 = sums[(2 * n) * Q:(2 * n + 1) * Q]
            h1 = sums[(2 * n + 1) * Q:(2 * n + 2) * Q]
            from_key = jnp.concatenate([h0[:, :K], h1[:, :K]], axis=1)
            total = jnp.concatenate([h0[:, K:], h1[:, K:]], axis=1)
            w = jnp.exp(logits[n] - from_key - carry)
            if diagonal:
                w = jnp.where(masks[n], w, 0.0)
            acc = acc + _dot(w.astype(BF16), per_head_rows(v_ref, g, kb))
            state[g] = (carry + total, acc)
        return tuple(state)

    per_q = Q // K

    def q_body(qi, _):
        q0 = pl.multiple_of(qi * Q, Q)
        qs = [q_ref[pl.ds(q0, Q), lanes[g]] for g in range(SB_GROUPS)]
        top = qi * per_q
        state = ((jnp.zeros((Q, 2 * K), F32), jnp.zeros((Q, V7X_LANES), F32)),) * SB_GROUPS
        state = sweep(qs, q0, [top + per_q - 1 - n for n in range(per_q)], state, True)
        rem = top % SB_UNROLL
        state = lax.fori_loop(
            0, rem // per_q,
            lambda j, st: sweep(qs, q0, [top - 1 - j * per_q - n for n in range(per_q)], st, False),
            state)
        base = top - rem
        state = lax.fori_loop(
            0, base // SB_UNROLL,
            lambda j, st: sweep(qs, q0, [base - 1 - j * SB_UNROLL - n for n in range(SB_UNROLL)],
                                st, False),
            state)
        for g in range(SB_GROUPS):
            o_ref[pl.ds(q0, Q), lanes[g]] = state[g][1].astype(o_ref.dtype)
        return 0

    lax.fori_loop(0, seq // Q, q_body, 0)


def _sb_constants():
    r = lax.broadcasted_iota(jnp.int32, (2 * SB_KEYS, 2 * SB_KEYS), 0) % SB_KEYS
    c = lax.broadcasted_iota(jnp.int32, (2 * SB_KEYS, 2 * SB_KEYS), 1)
    return jnp.where((c >= SB_KEYS) | (r >= c), 1.0, 0.0).astype(BF16)


def _stick_breaking(q, k, v, *, batch, seq):
    assert seq % SB_QUERIES == 0 and SB_QUERIES % SB_KEYS == 0
    assert SB_UNROLL % (SB_QUERIES // SB_KEYS) == 0 and 2 * SB_KEYS == V7X_MXU_DIM
    cs = _sb_constants()
    width = SB_GROUPS * V7X_LANES
    assert D_MODEL % width == 0
    group = pl.BlockSpec((None, seq, width), lambda b, g: (b, 0, g))
    resident = 2 * 4 * 2 * seq * width + 2 * cs.size \
        + 4 * 5 * SB_GROUPS * SB_UNROLL * SB_QUERIES * 2 * SB_KEYS
    return pl.pallas_call(
        functools.partial(_sb_kernel, seq=seq),
        grid=(batch, D_MODEL // width),
        in_specs=[group, group, group, _resident(cs.shape)],
        out_specs=group,
        out_shape=jax.ShapeDtypeStruct((batch, seq, D_MODEL), BF16),
        compiler_params=pltpu.CompilerParams(
            dimension_semantics=("arbitrary", "arbitrary"),
            vmem_limit_bytes=_vmem_limit(resident)),
        name="stick_breaking",
    )(q, k, v, cs)


def _proj_residual_kernel(a_ref, w_ref, x_ref, o_ref):
    o_ref[...] = x_ref[...] + _dot(a_ref[...], w_ref[...])


def _proj_residual(a, w, x):
    n = x.shape[0]
    row = pl.BlockSpec((PROJ_ROWS, D_MODEL), lambda i: (i, 0))
    resident = 2 * w.size + 2 * PROJ_ROWS * D_MODEL * (2 + 4 + 4) + 4 * PROJ_ROWS * D_MODEL
    return pl.pallas_call(
        _proj_residual_kernel,
        grid=(n // PROJ_ROWS,),
        in_specs=[row, _resident(w.shape), row],
        out_specs=row,
        out_shape=jax.ShapeDtypeStruct((n, D_MODEL), F32),
        compiler_params=pltpu.CompilerParams(
            dimension_semantics=("arbitrary",), vmem_limit_bytes=_vmem_limit(resident)),
        name="proj_residual",
    )(a, w, x)


def _rglru_kernel(x_ref, g_ref, win_ref, cw_ref, cb_ref, wg_ref, br_ref, bi_ref, lam_ref, wo_ref,
                  o_ref, xb_s, y_s, a_s, u_s, h_s, state_s, *, batch):
    rows = LRU_STEPS * batch
    halo = CONV_W * batch

    @pl.when(pl.program_id(0) == 0)
    def _():
        xb_s[0:halo, :] = jnp.zeros((halo, D_MODEL), F32)
        state_s[...] = jnp.zeros_like(state_s)

    x = x_ref[...]
    xn = _rms_norm_rows(x, g_ref[...]).astype(BF16)
    xb_s[halo:halo + rows, :] = _dot(xn, win_ref[:, 0:D_MODEL])
    y_s[...] = jax.nn.gelu(_dot(xn, win_ref[:, D_MODEL:2 * D_MODEL]), approximate=True)

    xc = cb_ref[...] + cw_ref[CONV_W - 1:CONV_W, :] * xb_s[halo:halo + rows, :]
    for j in range(CONV_W - 1):
        off = halo - (CONV_W - 1 - j) * batch
        xc = xc + cw_ref[j:j + 1, :] * xb_s[off:off + rows, :]
    xb_s[0:halo, :] = xb_s[rows:rows + halo, :]

    xc_b = xc.astype(BF16)
    r_parts, i_parts = [], []
    for g in range(D_MODEL // V7X_MXU_DIM):
        ri = _dot(xc_b[:, g * V7X_MXU_DIM:(g + 1) * V7X_MXU_DIM], wg_ref[g])
        r_parts.append(ri[:, :V7X_MXU_DIM])
        i_parts.append(ri[:, V7X_MXU_DIM:])
    r = jax.nn.sigmoid(jnp.concatenate(r_parts, axis=-1) + br_ref[...])
    i = jax.nn.sigmoid(jnp.concatenate(i_parts, axis=-1) + bi_ref[...])
    lam = lam_ref[...]
    log_sig_lam = jnp.minimum(lam, 0.0) - jnp.log1p(jnp.exp(-jnp.abs(lam)))
    log_a = (LRU_C * r) * log_sig_lam
    a_s[...] = jnp.exp(log_a)
    th = jnp.tanh(log_a)
    u_s[...] = jnp.sqrt(-2.0 * th / (1.0 - th)) * (i * xc)

    def step(t, h):
        sl = pl.ds(pl.multiple_of(t * batch, batch), batch)
        h = a_s[sl, :] * h + u_s[sl, :]
        h_s[sl, :] = h
        return h

    state_s[...] = lax.fori_loop(0, LRU_STEPS, step, state_s[...], unroll=8)
    o_ref[...] = x + _dot((h_s[...] * y_s[...]).astype(BF16), wo_ref[...])


def _block_diag_tiles(w):
    per = V7X_MXU_DIM // LRU_BLOCK_W
    w = w.reshape(LRU_BLOCKS // per, per, LRU_BLOCK_W, LRU_BLOCK_W)
    eye = jnp.eye(per, dtype=w.dtype)
    return jnp.einsum("gpcd,pq->gpcqd", w, eye).reshape(LRU_BLOCKS // per, V7X_MXU_DIM, V7X_MXU_DIM)


def _rglru(x, gain, w_in, conv_w, conv_b, w_r, b_r, w_i, b_i, lam, w_o, *, batch):
    assert batch == V7X_SUBLANES
    n = x.shape[0]
    rows = LRU_STEPS * batch
    w_gate = jnp.concatenate([_block_diag_tiles(w_r), _block_diag_tiles(w_i)], axis=-1).astype(BF16)
    vec = lambda p: p.reshape(1, D_MODEL)
    row = pl.BlockSpec((rows, D_MODEL), lambda i: (i, 0))
    resident = 2 * (w_in.size + w_gate.size + w_o.size) + 4 * 4 * rows * D_MODEL \
        + 4 * 5 * (rows + CONV_W * batch) * D_MODEL + 4 * 6 * rows * D_MODEL
    return pl.pallas_call(
        functools.partial(_rglru_kernel, batch=batch),
        grid=(n // rows,),
        in_specs=[row, _resident((1, D_MODEL)), _resident(w_in.shape),
                  _resident((CONV_W, D_MODEL)), _resident((1, D_MODEL)),
                  _resident(w_gate.shape), _resident((1, D_MODEL)), _resident((1, D_MODEL)),
                  _resident((1, D_MODEL)), _resident(w_o.shape)],
        out_specs=row,
        out_shape=jax.ShapeDtypeStruct((n, D_MODEL), F32),
        scratch_shapes=[pltpu.VMEM((rows + CONV_W * batch, D_MODEL), F32),
                        pltpu.VMEM((rows, D_MODEL), F32),
                        pltpu.VMEM((rows, D_MODEL), F32),
                        pltpu.VMEM((rows, D_MODEL), F32),
                        pltpu.VMEM((rows, D_MODEL), F32),
                        pltpu.VMEM((batch, D_MODEL), F32)],
        compiler_params=pltpu.CompilerParams(
            dimension_semantics=("arbitrary",), vmem_limit_bytes=_vmem_limit(resident)),
        name="rglru",
    )(x, vec(gain), w_in, conv_w, vec(conv_b), w_gate, vec(b_r), vec(b_i), vec(lam), w_o)


def kernel(x, l0_ff1_norm, l0_ff1_w_in, l0_ff1_w_out, l0_mix_norm, l0_sb_w_qkv, l0_sb_q_norm, l0_sb_k_norm, l0_sb_w_o, l0_ff2_norm, l0_ff2_w_in, l0_ff2_w_out, l1_ff1_norm, l1_ff1_w_in, l1_ff1_w_out, l1_mix_norm, l1_lru_w_in, l1_lru_conv_w, l1_lru_conv_b, l1_lru_w_r, l1_lru_b_r, l1_lru_w_i, l1_lru_b_i, l1_lru_lambda, l1_lru_w_o, l1_ff2_norm, l1_ff2_w_in, l1_ff2_w_out):
    batch, seq, d = x.shape
    assert d == D_MODEL
    n = batch * seq
    bf = lambda w: w.astype(BF16)
    ffn = functools.partial(_ffn, batch=batch, seq=seq)

    x = ffn(x, l0_ff1_norm, bf(l0_ff1_w_in), bf(l0_ff1_w_out), in_time_major=False, out_time_major=False)
    x2 = x.reshape(n, D_MODEL)
    q, k, v = _qkv(x2, l0_mix_norm, bf(l0_sb_w_qkv), l0_sb_q_norm, l0_sb_k_norm)
    as_seq = lambda t: t.reshape(batch, seq, D_MODEL)
    o = _stick_breaking(as_seq(q), as_seq(k), as_seq(v), batch=batch, seq=seq)
    x2 = _proj_residual(o.reshape(n, D_MODEL), bf(l0_sb_w_o), x2)
    x = ffn(as_seq(x2), l0_ff2_norm, bf(l0_ff2_w_in), bf(l0_ff2_w_out), in_time_major=False, out_time_major=False)

    x = ffn(x, l1_ff1_norm, bf(l1_ff1_w_in), bf(l1_ff1_w_out), in_time_major=False, out_time_major=True)
    x2 = _rglru(x.reshape(n, D_MODEL), l1_mix_norm, bf(l1_lru_w_in), l1_lru_conv_w, l1_lru_conv_b,
                l1_lru_w_r, l1_lru_b_r, l1_lru_w_i, l1_lru_b_i, l1_lru_lambda, bf(l1_lru_w_o), batch=batch)
    x = ffn(x2.reshape(seq, batch * D_MODEL), l1_ff2_norm, bf(l1_ff2_w_in), bf(l1_ff2_w_out),
            in_time_major=True, out_time_major=False)
    return x
```

```python
import functools
import math

import jax
import jax.numpy as jnp
from jax import lax
from jax.experimental import pallas as pl
from jax.experimental.pallas import tpu as pltpu

D_MODEL = 1024
SB_HEADS = 16
SB_HEAD_DIM = D_MODEL // SB_HEADS
LRU_BLOCKS = 16
LRU_BLOCK_W = D_MODEL // LRU_BLOCKS
LRU_C = 8.0
CONV_W = 4
NORM_EPS = 1e-6
LOG2_E = math.log2(math.e)

V7X_LANES = 128
V7X_SUBLANES = 8
V7X_MXU_DIM = 256
V7X_VMEM_BYTES = 64 * 1024 * 1024

F32 = jnp.float32
BF16 = jnp.bfloat16

FFN_ROWS = 512
FFN_CHUNK = V7X_MXU_DIM
PROJ_ROWS = 512
SB_QUERIES = 256
SB_KEYS = 128
SB_STEP = SB_QUERIES // SB_KEYS
SB_GROUPS = 2
SB_MASKED_LOGIT = -1e30
LRU_STEPS = 64


def _vmem_limit(resident_bytes):
    return int(min(resident_bytes * 3 // 2 + (4 << 20), V7X_VMEM_BYTES - (4 << 20)))


def _dot(a, b):
    return jnp.dot(a, b, preferred_element_type=F32)


def _rms_norm_rows(x, gain_row):
    ms = jnp.mean(x * x, axis=-1, keepdims=True)
    return x * lax.rsqrt(ms + NORM_EPS) * gain_row


def _split_bf16(v):
    hi = v.astype(BF16)
    lo = (v - hi.astype(F32)).astype(BF16)
    return hi, lo


def _resident(shape):
    zeros = (0,) * len(shape)
    return pl.BlockSpec(shape, lambda *_: zeros, pipeline_mode=pl.Buffered(1))


def _ffn_kernel(x_ref, g_ref, win_ref, wout_ref, *rest, d_ff):
    x = x_ref[...]
    if len(rest) == 3:
        a_ref, wmix_ref, o_ref = rest
        x = x + _dot(a_ref[...], wmix_ref[...])
    else:
        o_ref, = rest
    xn = _rms_norm_rows(x, g_ref[...]).astype(BF16)
    y = jnp.zeros_like(x)
    for c in range(d_ff // FFN_CHUNK):
        lo = c * FFN_CHUNK
        gate = _dot(xn, win_ref[:, lo:lo + FFN_CHUNK])
        up = _dot(xn, win_ref[:, d_ff + lo:d_ff + lo + FFN_CHUNK])
        h = (gate * jax.nn.sigmoid(gate) * up).astype(BF16)
        y = y + _dot(h, wout_ref[lo:lo + FFN_CHUNK, :])
    o_ref[...] = x + 0.5 * y


def _ffn(x, gain, w_in, w_out, mix=None):
    n = x.shape[0]
    d_ff = w_out.shape[0]
    assert d_ff % FFN_CHUNK == 0 and n % FFN_ROWS == 0
    row = pl.BlockSpec((FFN_ROWS, D_MODEL), lambda i: (i, 0))
    operands = [x, gain.reshape(1, D_MODEL), w_in, w_out]
    in_specs = [row, _resident((1, D_MODEL)), _resident(w_in.shape), _resident(w_out.shape)]
    resident = 2 * (w_in.size + w_out.size) + 4 * 4 * FFN_ROWS * D_MODEL \
        + 4 * FFN_ROWS * (3 * D_MODEL + 3 * FFN_CHUNK)
    if mix is not None:
        a, w_mix = mix
        operands += [a, w_mix]
        in_specs += [row, _resident(w_mix.shape)]
        resident += 2 * w_mix.size + 2 * 2 * FFN_ROWS * D_MODEL
    return pl.pallas_call(
        functools.partial(_ffn_kernel, d_ff=d_ff),
        grid=(n // FFN_ROWS,),
        in_specs=in_specs,
        out_specs=row,
        out_shape=jax.ShapeDtypeStruct((n, D_MODEL), F32),
        compiler_params=pltpu.CompilerParams(
            dimension_semantics=("arbitrary",), vmem_limit_bytes=_vmem_limit(resident)),
        name="ffn",
    )(*operands)


def _head_rms_norm(t, mean_ref, gain_row):
    outs = []
    for c in range(D_MODEL // V7X_MXU_DIM):
        tc = t[:, c * V7X_MXU_DIM:(c + 1) * V7X_MXU_DIM]
        hi, lo = _split_bf16(tc * tc)
        ms = _dot(hi, mean_ref[...]) + _dot(lo, mean_ref[...])
        outs.append(tc * lax.rsqrt(ms + NORM_EPS))
    return jnp.concatenate(outs, axis=-1) * gain_row


def _qkv_kernel(x_ref, g_ref, w_ref, mean_ref, qg_ref, kg_ref, q_ref, k_ref, v_ref):
    xn = _rms_norm_rows(x_ref[...], g_ref[...]).astype(BF16)
    q = _dot(xn, w_ref[:, 0:D_MODEL])
    q_ref[...] = (_head_rms_norm(q, mean_ref, qg_ref[...]) * (1.0 / math.sqrt(SB_HEAD_DIM))).astype(BF16)
    k = _dot(xn, w_ref[:, D_MODEL:2 * D_MODEL])
    k_ref[...] = _head_rms_norm(k, mean_ref, kg_ref[...]).astype(BF16)
    v_ref[...] = _dot(xn, w_ref[:, 2 * D_MODEL:3 * D_MODEL]).astype(BF16)


def _head_mean_matrix():
    r = lax.broadcasted_iota(jnp.int32, (V7X_MXU_DIM, V7X_MXU_DIM), 0) // SB_HEAD_DIM
    c = lax.broadcasted_iota(jnp.int32, (V7X_MXU_DIM, V7X_MXU_DIM), 1) // SB_HEAD_DIM
    return jnp.where(r == c, 1.0 / SB_HEAD_DIM, 0.0).astype(BF16)


def _qkv(x, gain, w_qkv, q_gain, k_gain):
    n = x.shape[0]
    row = pl.BlockSpec((PROJ_ROWS, D_MODEL), lambda i: (i, 0))
    out = jax.ShapeDtypeStruct((n, D_MODEL), BF16)
    resident = 2 * w_qkv.size + 4 * 2 * PROJ_ROWS * D_MODEL + 3 * 2 * 2 * PROJ_ROWS * D_MODEL \
        + 4 * 4 * PROJ_ROWS * D_MODEL
    tile_gain = lambda g: jnp.tile(g, SB_HEADS).reshape(1, D_MODEL)
    return pl.pallas_call(
        _qkv_kernel,
        grid=(n // PROJ_ROWS,),
        in_specs=[row, _resident((1, D_MODEL)), _resident(w_qkv.shape),
                  _resident((V7X_MXU_DIM, V7X_MXU_DIM)),
                  _resident((1, D_MODEL)), _resident((1, D_MODEL))],
        out_specs=[row, row, row],
        out_shape=[out, out, out],
        compiler_params=pltpu.CompilerParams(
            dimension_semantics=("arbitrary",), vmem_limit_bytes=_vmem_limit(resident)),
        name="qkv",
    )(x, gain.reshape(1, D_MODEL), w_qkv, _head_mean_matrix(), tile_gain(q_gain), tile_gain(k_gain))


def _sb_kernel(q_ref, k_ref, v_ref, cs_ref, o_ref, kh_s, vh_s, x_s, total_s, carry_s, acc_s, *, seq):
    Q, K = SB_QUERIES, SB_KEYS
    lanes = [slice(g * V7X_LANES, (g + 1) * V7X_LANES) for g in range(SB_GROUPS)]

    lane = lax.broadcasted_iota(jnp.int32, (K, V7X_LANES), 1)
    head_lanes = [jnp.where((lane < SB_HEAD_DIM) == (h == 0), 1.0, 0.0).astype(BF16) for h in range(2)]

    def head_rows(kb, count=1):
        return pl.ds(pl.multiple_of(kb * 2 * K, 2 * K), count * 2 * K)

    def split_heads(kb, _):
        rows = pl.ds(pl.multiple_of(kb * K, K), K)
        for g in range(SB_GROUPS):
            for src, dst in ((k_ref, kh_s), (v_ref, vh_s)):
                blk = src[rows, lanes[g]]
                dst[g, head_rows(kb), :] = jnp.concatenate(
                    [blk * head_lanes[0], blk * head_lanes[1]], axis=0)
        return 0

    lax.fori_loop(0, seq // K, split_heads, 0)

    def score(qs, q0, kbs, diagonal):
        work = [(g, kb) for g in range(SB_GROUPS) for kb in kbs]
        logits, s_parts = [], []
        for g, kb in work:
            z = lax.dot_general(qs[g], kh_s[g, head_rows(kb), :], (((1,), (1,)), ((), ())),
                                preferred_element_type=F32)
            s = jnp.maximum(z, 0.0) + jnp.log(1.0 + jnp.exp2(jnp.abs(z) * -LOG2_E))
            if diagonal:
                row = lax.broadcasted_iota(jnp.int32, (Q, 2 * K), 0)
                col = lax.broadcasted_iota(jnp.int32, (Q, 2 * K), 1)
                earlier = (kb * K + (col & (K - 1))) < (q0 + row)
                s = jnp.where(earlier, s, 0.0)
                z = jnp.where(earlier, z, SB_MASKED_LOGIT)
            s_hi, s_lo = _split_bf16(s)
            for h in range(2):
                s_parts.append(jnp.concatenate([s_hi[:, h * K:(h + 1) * K],
                                                s_lo[:, h * K:(h + 1) * K]], axis=1))
            logits.append(z)
        sums = _dot(jnp.concatenate(s_parts, axis=0), cs_ref[...])
        for n, (g, kb) in enumerate(work):
            i = n % len(kbs)
            h0 = sums[(2 * n) * Q:(2 * n + 1) * Q]
            h1 = sums[(2 * n + 1) * Q:(2 * n + 2) * Q]
            from_key = jnp.concatenate([h0[:, :K], h1[:, :K]], axis=1)
            total = jnp.concatenate([h0[:, K:], h1[:, K:]], axis=1)
            x_s[g, i] = logits[n] - from_key if i == 0 else logits[n] - from_key - later
            later = total if i == 0 else later + total
            if i == len(kbs) - 1:
                total_s[g] = later

    def fold(kbs):
        for g in range(SB_GROUPS):
            carry = carry_s[g]
            w = [jnp.exp(x_s[g, i] - carry).astype(BF16) for i in reversed(range(len(kbs)))]
            values = vh_s[g, head_rows(kbs[-1], len(kbs)), :]
            acc_s[g] += _dot(jnp.concatenate(w, axis=1), values)
            carry_s[g] = carry + total_s[g]

    def q_body(qi, _):
        q0 = pl.multiple_of(qi * Q, Q)
        qs = [q_ref[pl.ds(q0, Q), lanes[g]] for g in range(SB_GROUPS)]
        top = (qi + 1) * SB_STEP - 1
        group = lambda j: [top - j * SB_STEP - n for n in range(SB_STEP)]
        carry_s[...] = jnp.zeros_like(carry_s)
        acc_s[...] = jnp.zeros_like(acc_s)
        score(qs, q0, group(0), True)

        def trip(j, _):
            fold(group(j))
            score(qs, q0, group(j + 1), False)
            return 0

        lax.fori_loop(0, qi, trip, 0)
        fold(group(qi))
        for g in range(SB_GROUPS):
            o_ref[pl.ds(q0, Q), lanes[g]] = acc_s[g].astype(o_ref.dtype)
        return 0

    lax.fori_loop(0, seq // Q, q_body, 0)


def _sb_constants():
    r = lax.broadcasted_iota(jnp.int32, (2 * SB_KEYS, 2 * SB_KEYS), 0) % SB_KEYS
    c = lax.broadcasted_iota(jnp.int32, (2 * SB_KEYS, 2 * SB_KEYS), 1)
    return jnp.where((c >= SB_KEYS) | (r >= c), 1.0, 0.0).astype(BF16)


def _stick_breaking(q, k, v, *, batch, seq):
    assert seq % SB_QUERIES == 0 and SB_QUERIES == SB_STEP * SB_KEYS and 2 * SB_KEYS == V7X_MXU_DIM
    cs = _sb_constants()
    width = SB_GROUPS * V7X_LANES
    assert D_MODEL % width == 0
    group = pl.BlockSpec((None, seq, width), lambda b, g: (b, 0, g))
    per_head = pltpu.VMEM((SB_GROUPS, 2 * seq, V7X_LANES), BF16)
    scores = (SB_GROUPS, SB_QUERIES, 2 * SB_KEYS)
    resident = 2 * 4 * 2 * seq * width + 2 * cs.size + 2 * 2 * 2 * seq * width \
        + 4 * 8 * SB_GROUPS * SB_STEP * SB_QUERIES * 2 * SB_KEYS
    return pl.pallas_call(
        functools.partial(_sb_kernel, seq=seq),
        grid=(batch, D_MODEL // width),
        in_specs=[group, group, group, _resident(cs.shape)],
        out_specs=group,
        out_shape=jax.ShapeDtypeStruct((batch, seq, D_MODEL), BF16),
        scratch_shapes=[per_head, per_head,
                        pltpu.VMEM((SB_GROUPS, SB_STEP) + scores[1:], F32),
                        pltpu.VMEM(scores, F32), pltpu.VMEM(scores, F32),
                        pltpu.VMEM((SB_GROUPS, SB_QUERIES, V7X_LANES), F32)],
        compiler_params=pltpu.CompilerParams(
            dimension_semantics=("arbitrary", "arbitrary"),
            vmem_limit_bytes=_vmem_limit(resident)),
        name="stick_breaking",
    )(q, k, v, cs)


def _rglru_kernel(x_ref, g_ref, win_ref, cw_ref, cb_ref, wg_ref, br_ref, bi_ref, lam_ref, wo_ref,
                  o_ref, xt_s, ot_s, xb_s, y_s, a_s, u_s, h_s, state_s, *, batch):
    rows = LRU_STEPS * batch
    halo = CONV_W * batch
    lane_tiles = D_MODEL // V7X_LANES

    @pl.when(pl.program_id(0) == 0)
    def _():
        xb_s[0:halo, :] = jnp.zeros((halo, D_MODEL), F32)
        state_s[...] = jnp.zeros_like(state_s)

    for b in range(batch):
        for l in range(lane_tiles):
            xt_s[l, pl.ds(b, LRU_STEPS, stride=batch), :] = \
                x_ref[b, :, l * V7X_LANES:(l + 1) * V7X_LANES]
    x = jnp.concatenate([xt_s[l] for l in range(lane_tiles)], axis=1)
    xn = _rms_norm_rows(x, g_ref[...]).astype(BF16)
    xb_s[halo:halo + rows, :] = _dot(xn, win_ref[:, 0:D_MODEL])
    y_s[...] = jax.nn.gelu(_dot(xn, win_ref[:, D_MODEL:2 * D_MODEL]), approximate=True)

    xc = cb_ref[...] + cw_ref[CONV_W - 1:CONV_W, :] * xb_s[halo:halo + rows, :]
    for j in range(CONV_W - 1):
        off = halo - (CONV_W - 1 - j) * batch
        xc = xc + cw_ref[j:j + 1, :] * xb_s[off:off + rows, :]
    xb_s[0:halo, :] = xb_s[rows:rows + halo, :]

    xc_b = xc.astype(BF16)
    r_parts, i_parts = [], []
    for g in range(D_MODEL // V7X_MXU_DIM):
        ri = _dot(xc_b[:, g * V7X_MXU_DIM:(g + 1) * V7X_MXU_DIM], wg_ref[g])
        r_parts.append(ri[:, :V7X_MXU_DIM])
        i_parts.append(ri[:, V7X_MXU_DIM:])
    r = jax.nn.sigmoid(jnp.concatenate(r_parts, axis=-1) + br_ref[...])
    i = jax.nn.sigmoid(jnp.concatenate(i_parts, axis=-1) + bi_ref[...])
    lam = lam_ref[...]
    log_sig_lam = jnp.minimum(lam, 0.0) - jnp.log1p(jnp.exp(-jnp.abs(lam)))
    log_a = (LRU_C * r) * log_sig_lam
    a_s[...] = jnp.exp(log_a)
    th = jnp.tanh(log_a)
    u_s[...] = jnp.sqrt(-2.0 * th / (1.0 - th)) * (i * xc)

    def step(t, h):
        sl = pl.ds(pl.multiple_of(t * batch, batch), batch)
        h = a_s[sl, :] * h + u_s[sl, :]
        h_s[sl, :] = h
        return h

    state_s[...] = lax.fori_loop(0, LRU_STEPS, step, state_s[...], unroll=8)
    out = x + _dot((h_s[...] * y_s[...]).astype(BF16), wo_ref[...])
    for l in range(lane_tiles):
        ot_s[l] = out[:, l * V7X_LANES:(l + 1) * V7X_LANES]
    for b in range(batch):
        for l in range(lane_tiles):
            o_ref[b, :, l * V7X_LANES:(l + 1) * V7X_LANES] = \
                ot_s[l, pl.ds(b, LRU_STEPS, stride=batch), :]


def _block_diag_tiles(w):
    per = V7X_MXU_DIM // LRU_BLOCK_W
    w = w.reshape(LRU_BLOCKS // per, per, LRU_BLOCK_W, LRU_BLOCK_W)
    eye = jnp.eye(per, dtype=w.dtype)
    return jnp.einsum("gpcd,pq->gpcqd", w, eye).reshape(LRU_BLOCKS // per, V7X_MXU_DIM, V7X_MXU_DIM)


def _rglru(x, gain, w_in, conv_w, conv_b, w_r, b_r, w_i, b_i, lam, w_o):
    batch, seq, _ = x.shape
    assert batch == V7X_SUBLANES and seq % LRU_STEPS == 0
    rows = LRU_STEPS * batch
    w_gate = jnp.concatenate([_block_diag_tiles(w_r), _block_diag_tiles(w_i)], axis=-1).astype(BF16)
    vec = lambda p: p.reshape(1, D_MODEL)
    chunk = pl.BlockSpec((batch, LRU_STEPS, D_MODEL), lambda i: (0, i, 0))
    slabs = pltpu.VMEM((D_MODEL // V7X_LANES, rows, V7X_LANES), F32)
    resident = 2 * (w_in.size + w_gate.size + w_o.size) + 4 * 4 * rows * D_MODEL \
        + 4 * 7 * (rows + CONV_W * batch) * D_MODEL + 4 * 6 * rows * D_MODEL
    return pl.pallas_call(
        functools.partial(_rglru_kernel, batch=batch),
        grid=(seq // LRU_STEPS,),
        in_specs=[chunk, _resident((1, D_MODEL)), _resident(w_in.shape),
                  _resident((CONV_W, D_MODEL)), _resident((1, D_MODEL)),
                  _resident(w_gate.shape), _resident((1, D_MODEL)), _resident((1, D_MODEL)),
                  _resident((1, D_MODEL)), _resident(w_o.shape)],
        out_specs=chunk,
        out_shape=jax.ShapeDtypeStruct(x.shape, F32),
        scratch_shapes=[slabs, slabs,
                        pltpu.VMEM((rows + CONV_W * batch, D_MODEL), F32),
                        pltpu.VMEM((rows, D_MODEL), F32),
                        pltpu.VMEM((rows, D_MODEL), F32),
                        pltpu.VMEM((rows, D_MODEL), F32),
                        pltpu.VMEM((rows, D_MODEL), F32),
                        pltpu.VMEM((batch, D_MODEL), F32)],
        compiler_params=pltpu.CompilerParams(
            dimension_semantics=("arbitrary",), vmem_limit_bytes=_vmem_limit(resident)),
        name="rglru",
    )(x, vec(gain), w_in, conv_w, vec(conv_b), w_gate, vec(b_r), vec(b_i), vec(lam), w_o)


def kernel(x, l0_ff1_norm, l0_ff1_w_in, l0_ff1_w_out, l0_mix_norm, l0_sb_w_qkv, l0_sb_q_norm, l0_sb_k_norm, l0_sb_w_o, l0_ff2_norm, l0_ff2_w_in, l0_ff2_w_out, l1_ff1_norm, l1_ff1_w_in, l1_ff1_w_out, l1_mix_norm, l1_lru_w_in, l1_lru_conv_w, l1_lru_conv_b, l1_lru_w_r, l1_lru_b_r, l1_lru_w_i, l1_lru_b_i, l1_lru_lambda, l1_lru_w_o, l1_ff2_norm, l1_ff2_w_in, l1_ff2_w_out):
    batch, seq, d = x.shape
    assert d == D_MODEL
    n = batch * seq
    bf = lambda w: w.astype(BF16)
    as_seq = lambda t: t.reshape(batch, seq, D_MODEL)
    x = x.reshape(n, D_MODEL)

    x = _ffn(x, l0_ff1_norm, bf(l0_ff1_w_in), bf(l0_ff1_w_out))
    q, k, v = _qkv(x, l0_mix_norm, bf(l0_sb_w_qkv), l0_sb_q_norm, l0_sb_k_norm)
    o = _stick_breaking(as_seq(q), as_seq(k), as_seq(v), batch=batch, seq=seq)
    x = _ffn(x, l0_ff2_norm, bf(l0_ff2_w_in), bf(l0_ff2_w_out),
             mix=(o.reshape(n, D_MODEL), bf(l0_sb_w_o)))

    x = _ffn(x, l1_ff1_norm, bf(l1_ff1_w_in), bf(l1_ff1_w_out))
    x = _rglru(as_seq(x), l1_mix_norm, bf(l1_lru_w_in), l1_lru_conv_w, l1_lru_conv_b,
               l1_lru_w_r, l1_lru_b_r, l1_lru_w_i, l1_lru_b_i, l1_lru_lambda, bf(l1_lru_w_o))
    x = _ffn(x.reshape(n, D_MODEL), l1_ff2_norm, bf(l1_ff2_w_in), bf(l1_ff2_w_out))
    return as_seq(x)
```

```python
import functools
import math

import jax
import jax.numpy as jnp
from jax import lax
from jax.experimental import pallas as pl
from jax.experimental.pallas import tpu as pltpu

D_MODEL = 1024
SB_HEADS = 16
SB_HEAD_DIM = D_MODEL // SB_HEADS
LRU_BLOCKS = 16
LRU_BLOCK_W = D_MODEL // LRU_BLOCKS
LRU_C = 8.0
CONV_W = 4
NORM_EPS = 1e-6
LOG2_E = math.log2(math.e)

V7X_LANES = 128
V7X_SUBLANES = 8
V7X_MXU_DIM = 256
V7X_VMEM_BYTES = 64 * 1024 * 1024

F32 = jnp.float32
BF16 = jnp.bfloat16

FFN_ROWS = 512
FFN_CHUNK = V7X_MXU_DIM
PROJ_ROWS = 512
SB_QUERIES = 256
SB_KEYS = 128
SB_STEP = SB_QUERIES // SB_KEYS
SB_GROUPS = 4
SB_MASKED_LOGIT = -1e30
SB_DEAD_CARRY = 128.0
LRU_STEPS = 64


def _vmem_limit(resident_bytes):
    return int(min(resident_bytes * 3 // 2 + (4 << 20), V7X_VMEM_BYTES - (4 << 20)))


def _dot(a, b):
    return jnp.dot(a, b, preferred_element_type=F32)


def _rms_norm_rows(x, gain_row):
    ms = jnp.mean(x * x, axis=-1, keepdims=True)
    return x * lax.rsqrt(ms + NORM_EPS) * gain_row


def _split_bf16(v):
    hi = v.astype(BF16)
    lo = (v - hi.astype(F32)).astype(BF16)
    return hi, lo


def _resident(shape):
    zeros = (0,) * len(shape)
    return pl.BlockSpec(shape, lambda *_: zeros, pipeline_mode=pl.Buffered(1))


def _ffn_kernel(x_ref, g_ref, win_ref, wout_ref, *rest, d_ff):
    x = x_ref[...]
    if len(rest) == 3:
        a_ref, wmix_ref, o_ref = rest
        x = x + _dot(a_ref[...], wmix_ref[...])
    else:
        o_ref, = rest
    xn = _rms_norm_rows(x, g_ref[...]).astype(BF16)
    y = jnp.zeros_like(x)
    for c in range(d_ff // FFN_CHUNK):
        lo = c * FFN_CHUNK
        gate = _dot(xn, win_ref[:, lo:lo + FFN_CHUNK])
        up = _dot(xn, win_ref[:, d_ff + lo:d_ff + lo + FFN_CHUNK])
        h = (gate * jax.nn.sigmoid(gate) * up).astype(BF16)
        y = y + _dot(h, wout_ref[lo:lo + FFN_CHUNK, :])
    o_ref[...] = x + 0.5 * y


def _ffn(x, gain, w_in, w_out, mix=None):
    n = x.shape[0]
    d_ff = w_out.shape[0]
    assert d_ff % FFN_CHUNK == 0 and n % FFN_ROWS == 0
    row = pl.BlockSpec((FFN_ROWS, D_MODEL), lambda i: (i, 0))
    operands = [x, gain.reshape(1, D_MODEL), w_in, w_out]
    in_specs = [row, _resident((1, D_MODEL)), _resident(w_in.shape), _resident(w_out.shape)]
    resident = 2 * (w_in.size + w_out.size) + 4 * 4 * FFN_ROWS * D_MODEL \
        + 4 * FFN_ROWS * (3 * D_MODEL + 3 * FFN_CHUNK)
    if mix is not None:
        a, w_mix = mix
        operands += [a, w_mix]
        in_specs += [row, _resident(w_mix.shape)]
        resident += 2 * w_mix.size + 2 * 2 * FFN_ROWS * D_MODEL
    return pl.pallas_call(
        functools.partial(_ffn_kernel, d_ff=d_ff),
        grid=(n // FFN_ROWS,),
        in_specs=in_specs,
        out_specs=row,
        out_shape=jax.ShapeDtypeStruct((n, D_MODEL), F32),
        compiler_params=pltpu.CompilerParams(
            dimension_semantics=("arbitrary",), vmem_limit_bytes=_vmem_limit(resident)),
        name="ffn",
    )(*operands)


def _head_rms_norm(t, mean_ref, gain_row):
    outs = []
    for c in range(D_MODEL // V7X_MXU_DIM):
        tc = t[:, c * V7X_MXU_DIM:(c + 1) * V7X_MXU_DIM]
        hi, lo = _split_bf16(tc * tc)
        ms = _dot(hi, mean_ref[...]) + _dot(lo, mean_ref[...])
        outs.append(tc * lax.rsqrt(ms + NORM_EPS))
    return jnp.concatenate(outs, axis=-1) * gain_row


def _qkv_kernel(x_ref, g_ref, w_ref, mean_ref, qg_ref, kg_ref, q_ref, k_ref, v_ref):
    xn = _rms_norm_rows(x_ref[...], g_ref[...]).astype(BF16)
    q = _dot(xn, w_ref[:, 0:D_MODEL])
    q_ref[...] = (_head_rms_norm(q, mean_ref, qg_ref[...]) * (1.0 / math.sqrt(SB_HEAD_DIM))).astype(BF16)
    k = _dot(xn, w_ref[:, D_MODEL:2 * D_MODEL])
    k_ref[...] = _head_rms_norm(k, mean_ref, kg_ref[...]).astype(BF16)
    v_ref[...] = _dot(xn, w_ref[:, 2 * D_MODEL:3 * D_MODEL]).astype(BF16)


def _head_mean_matrix():
    r = lax.broadcasted_iota(jnp.int32, (V7X_MXU_DIM, V7X_MXU_DIM), 0) // SB_HEAD_DIM
    c = lax.broadcasted_iota(jnp.int32, (V7X_MXU_DIM, V7X_MXU_DIM), 1) // SB_HEAD_DIM
    return jnp.where(r == c, 1.0 / SB_HEAD_DIM, 0.0).astype(BF16)


def _qkv(x, gain, w_qkv, q_gain, k_gain):
    n = x.shape[0]
    row = pl.BlockSpec((PROJ_ROWS, D_MODEL), lambda i: (i, 0))
    out = jax.ShapeDtypeStruct((n, D_MODEL), BF16)
    resident = 2 * w_qkv.size + 4 * 2 * PROJ_ROWS * D_MODEL + 3 * 2 * 2 * PROJ_ROWS * D_MODEL \
        + 4 * 4 * PROJ_ROWS * D_MODEL
    tile_gain = lambda g: jnp.tile(g, SB_HEADS).reshape(1, D_MODEL)
    return pl.pallas_call(
        _qkv_kernel,
        grid=(n // PROJ_ROWS,),
        in_specs=[row, _resident((1, D_MODEL)), _resident(w_qkv.shape),
                  _resident((V7X_MXU_DIM, V7X_MXU_DIM)),
                  _resident((1, D_MODEL)), _resident((1, D_MODEL))],
        out_specs=[row, row, row],
        out_shape=[out, out, out],
        compiler_params=pltpu.CompilerParams(
            dimension_semantics=("arbitrary",), vmem_limit_bytes=_vmem_limit(resident)),
        name="qkv",
    )(x, gain.reshape(1, D_MODEL), w_qkv, _head_mean_matrix(), tile_gain(q_gain), tile_gain(k_gain))


def _sb_kernel(q_ref, k_ref, v_ref, cs_ref, o_ref, kh_s, vh_s, x_s, total_s, carry_s, acc_s, *, seq):
    Q, K = SB_QUERIES, SB_KEYS
    lanes = [slice(g * V7X_LANES, (g + 1) * V7X_LANES) for g in range(SB_GROUPS)]

    lane = lax.broadcasted_iota(jnp.int32, (K, V7X_LANES), 1)
    head_lanes = [jnp.where((lane < SB_HEAD_DIM) == (h == 0), 1.0, 0.0).astype(BF16) for h in range(2)]

    def head_rows(kb, count=1):
        return pl.ds(pl.multiple_of(kb * 2 * K, 2 * K), count * 2 * K)

    def split_heads(kb, _):
        rows = pl.ds(pl.multiple_of(kb * K, K), K)
        for g in range(SB_GROUPS):
            for src, dst in ((k_ref, kh_s), (v_ref, vh_s)):
                blk = src[rows, lanes[g]]
                dst[g, head_rows(kb), :] = jnp.concatenate(
                    [blk * head_lanes[0], blk * head_lanes[1]], axis=0)
        return 0

    lax.fori_loop(0, seq // K, split_heads, 0)

    def score(qs, q0, kbs, diagonal):
        work = [(g, kb) for g in range(SB_GROUPS) for kb in kbs]
        logits, s_parts = [], []
        for g, kb in work:
            z = lax.dot_general(qs[g], kh_s[g, head_rows(kb), :], (((1,), (1,)), ((), ())),
                                preferred_element_type=F32)
            s = jnp.maximum(z, 0.0) + jnp.log(1.0 + jnp.exp2(jnp.abs(z) * -LOG2_E))
            if diagonal:
                row = lax.broadcasted_iota(jnp.int32, (Q, 2 * K), 0)
                col = lax.broadcasted_iota(jnp.int32, (Q, 2 * K), 1)
                earlier = (kb * K + (col & (K - 1))) < (q0 + row)
                s = jnp.where(earlier, s, 0.0)
                z = jnp.where(earlier, z, SB_MASKED_LOGIT)
            s_hi, s_lo = _split_bf16(s)
            for h in range(2):
                s_parts.append(jnp.concatenate([s_hi[:, h * K:(h + 1) * K],
                                                s_lo[:, h * K:(h + 1) * K]], axis=1))
            logits.append(z)
        sums = _dot(jnp.concatenate(s_parts, axis=0), cs_ref[...])
        for n, (g, kb) in enumerate(work):
            i = n % len(kbs)
            h0 = sums[(2 * n) * Q:(2 * n + 1) * Q]
            h1 = sums[(2 * n + 1) * Q:(2 * n + 2) * Q]
            from_key = jnp.concatenate([h0[:, :K], h1[:, :K]], axis=1)
            total = jnp.concatenate([h0[:, K:], h1[:, K:]], axis=1)
            x_s[g, i] = logits[n] - from_key if i == 0 else logits[n] - from_key - later
            later = total if i == 0 else later + total
            if i == len(kbs) - 1:
                total_s[g] = later

    def fold(kbs):
        smallest = None
        for g in range(SB_GROUPS):
            carry = carry_s[g]
            w = [jnp.exp(x_s[g, i] - carry).astype(BF16) for i in reversed(range(len(kbs)))]
            values = vh_s[g, head_rows(kbs[-1], len(kbs)), :]
            acc_s[g] += _dot(jnp.concatenate(w, axis=1), values)
            carry = carry + total_s[g]
            carry_s[g] = carry
            smallest = carry if smallest is None else jnp.minimum(smallest, carry)
        return jnp.min(smallest)

    def q_body(qi, _):
        q0 = pl.multiple_of(qi * Q, Q)
        qs = [q_ref[pl.ds(q0, Q), lanes[g]] for g in range(SB_GROUPS)]
        top = (qi + 1) * SB_STEP - 1
        group = lambda j: [top - j * SB_STEP - n for n in range(SB_STEP)]
        carry_s[...] = jnp.zeros_like(carry_s)
        acc_s[...] = jnp.zeros_like(acc_s)
        score(qs, q0, group(0), True)

        def trip(state):
            j, _ = state
            smallest = fold(group(j))
            score(qs, q0, group(j + 1), False)
            return j + 1, smallest <= SB_DEAD_CARRY

        j, live = lax.while_loop(lambda state: (state[0] < qi) & state[1], trip,
                                 (jnp.int32(0), jnp.bool_(True)))

        @pl.when(live)
        def _():
            fold(group(qi))

        for g in range(SB_GROUPS):
            o_ref[pl.ds(q0, Q), lanes[g]] = acc_s[g].astype(o_ref.dtype)
        return 0

    lax.fori_loop(0, seq // Q, q_body, 0)


def _sb_constants():
    r = lax.broadcasted_iota(jnp.int32, (2 * SB_KEYS, 2 * SB_KEYS), 0) % SB_KEYS
    c = lax.broadcasted_iota(jnp.int32, (2 * SB_KEYS, 2 * SB_KEYS), 1)
    return jnp.where((c >= SB_KEYS) | (r >= c), 1.0, 0.0).astype(BF16)


def _stick_breaking(q, k, v, *, batch, seq):
    assert seq % SB_QUERIES == 0 and SB_QUERIES == SB_STEP * SB_KEYS and 2 * SB_KEYS == V7X_MXU_DIM
    cs = _sb_constants()
    width = SB_GROUPS * V7X_LANES
    assert D_MODEL % width == 0
    group = pl.BlockSpec((None, seq, width), lambda b, g: (b, 0, g))
    per_head = pltpu.VMEM((SB_GROUPS, 2 * seq, V7X_LANES), BF16)
    scores = (SB_GROUPS, SB_QUERIES, 2 * SB_KEYS)
    resident = 2 * 4 * 2 * seq * width + 2 * cs.size + 2 * 2 * 2 * seq * width \
        + 4 * 8 * SB_GROUPS * SB_STEP * SB_QUERIES * 2 * SB_KEYS
    return pl.pallas_call(
        functools.partial(_sb_kernel, seq=seq),
        grid=(batch, D_MODEL // width),
        in_specs=[group, group, group, _resident(cs.shape)],
        out_specs=group,
        out_shape=jax.ShapeDtypeStruct((batch, seq, D_MODEL), BF16),
        scratch_shapes=[per_head, per_head,
                        pltpu.VMEM((SB_GROUPS, SB_STEP) + scores[1:], F32),
                        pltpu.VMEM(scores, F32), pltpu.VMEM(scores, F32),
                        pltpu.VMEM((SB_GROUPS, SB_QUERIES, V7X_LANES), F32)],
        compiler_params=pltpu.CompilerParams(
            dimension_semantics=("arbitrary", "arbitrary"),
            vmem_limit_bytes=_vmem_limit(resident)),
        name="stick_breaking",
    )(q, k, v, cs)


def _rglru_kernel(x_ref, g_ref, win_ref, cw_ref, cb_ref, wg_ref, br_ref, bi_ref, lam_ref, wo_ref,
                  o_ref, xt_s, ot_s, xb_s, y_s, a_s, u_s, h_s, state_s, *, batch):
    rows = LRU_STEPS * batch
    halo = CONV_W * batch
    lane_tiles = D_MODEL // V7X_LANES

    @pl.when(pl.program_id(0) == 0)
    def _():
        xb_s[0:halo, :] = jnp.zeros((halo, D_MODEL), F32)
        state_s[...] = jnp.zeros_like(state_s)

    for b in range(batch):
        for l in range(lane_tiles):
            xt_s[l, pl.ds(b, LRU_STEPS, stride=batch), :] = \
                x_ref[b, :, l * V7X_LANES:(l + 1) * V7X_LANES]
    x = jnp.concatenate([xt_s[l] for l in range(lane_tiles)], axis=1)
    xn = _rms_norm_rows(x, g_ref[...]).astype(BF16)
    xb_s[halo:halo + rows, :] = _dot(xn, win_ref[:, 0:D_MODEL])
    y_s[...] = jax.nn.gelu(_dot(xn, win_ref[:, D_MODEL:2 * D_MODEL]), approximate=True)

    xc = cb_ref[...] + cw_ref[CONV_W - 1:CONV_W, :] * xb_s[halo:halo + rows, :]
    for j in range(CONV_W - 1):
        off = halo - (CONV_W - 1 - j) * batch
        xc = xc + cw_ref[j:j + 1, :] * xb_s[off:off + rows, :]
    xb_s[0:halo, :] = xb_s[rows:rows + halo, :]

    xc_b = xc.astype(BF16)
    r_parts, i_parts = [], []
    for g in range(D_MODEL // V7X_MXU_DIM):
        ri = _dot(xc_b[:, g * V7X_MXU_DIM:(g + 1) * V7X_MXU_DIM], wg_ref[g])
        r_parts.append(ri[:, :V7X_MXU_DIM])
        i_parts.append(ri[:, V7X_MXU_DIM:])
    r = jax.nn.sigmoid(jnp.concatenate(r_parts, axis=-1) + br_ref[...])
    i = jax.nn.sigmoid(jnp.concatenate(i_parts, axis=-1) + bi_ref[...])
    lam = lam_ref[...]
    log_sig_lam = jnp.minimum(lam, 0.0) - jnp.log1p(jnp.exp(-jnp.abs(lam)))
    log_a = (LRU_C * r) * log_sig_lam
    a_s[...] = jnp.exp(log_a)
    th = jnp.tanh(log_a)
    u_s[...] = jnp.sqrt(-2.0 * th / (1.0 - th)) * (i * xc)

    def step(t, h):
        sl = pl.ds(pl.multiple_of(t * batch, batch), batch)
        h = a_s[sl, :] * h + u_s[sl, :]
        h_s[sl, :] = h
        return h

    state_s[...] = lax.fori_loop(0, LRU_STEPS, step, state_s[...], unroll=8)
    out = x + _dot((h_s[...] * y_s[...]).astype(BF16), wo_ref[...])
    for l in range(lane_tiles):
        ot_s[l] = out[:, l * V7X_LANES:(l + 1) * V7X_LANES]
    for b in range(batch):
        for l in range(lane_tiles):
            o_ref[b, :, l * V7X_LANES:(l + 1) * V7X_LANES] = \
                ot_s[l, pl.ds(b, LRU_STEPS, stride=batch), :]


def _block_diag_tiles(w):
    per = V7X_MXU_DIM // LRU_BLOCK_W
    w = w.reshape(LRU_BLOCKS // per, per, LRU_BLOCK_W, LRU_BLOCK_W)
    eye = jnp.eye(per, dtype=w.dtype)
    return jnp.einsum("gpcd,pq->gpcqd", w, eye).reshape(LRU_BLOCKS // per, V7X_MXU_DIM, V7X_MXU_DIM)


def _rglru(x, gain, w_in, conv_w, conv_b, w_r, b_r, w_i, b_i, lam, w_o):
    batch, seq, _ = x.shape
    assert batch == V7X_SUBLANES and seq % LRU_STEPS == 0
    rows = LRU_STEPS * batch
    w_gate = jnp.concatenate([_block_diag_tiles(w_r), _block_diag_tiles(w_i)], axis=-1).astype(BF16)
    vec = lambda p: p.reshape(1, D_MODEL)
    chunk = pl.BlockSpec((batch, LRU_STEPS, D_MODEL), lambda i: (0, i, 0))
    slabs = pltpu.VMEM((D_MODEL // V7X_LANES, rows, V7X_LANES), F32)
    resident = 2 * (w_in.size + w_gate.size + w_o.size) + 4 * 4 * rows * D_MODEL \
        + 4 * 7 * (rows + CONV_W * batch) * D_MODEL + 4 * 6 * rows * D_MODEL
    return pl.pallas_call(
        functools.partial(_rglru_kernel, batch=batch),
        grid=(seq // LRU_STEPS,),
        in_specs=[chunk, _resident((1, D_MODEL)), _resident(w_in.shape),
                  _resident((CONV_W, D_MODEL)), _resident((1, D_MODEL)),
                  _resident(w_gate.shape), _resident((1, D_MODEL)), _resident((1, D_MODEL)),
                  _resident((1, D_MODEL)), _resident(w_o.shape)],
        out_specs=chunk,
        out_shape=jax.ShapeDtypeStruct(x.shape, F32),
        scratch_shapes=[slabs, slabs,
                        pltpu.VMEM((rows + CONV_W * batch, D_MODEL), F32),
                        pltpu.VMEM((rows, D_MODEL), F32),
                        pltpu.VMEM((rows, D_MODEL), F32),
                        pltpu.VMEM((rows, D_MODEL), F32),
                        pltpu.VMEM((rows, D_MODEL), F32),
                        pltpu.VMEM((batch, D_MODEL), F32)],
        compiler_params=pltpu.CompilerParams(
            dimension_semantics=("arbitrary",), vmem_limit_bytes=_vmem_limit(resident)),
        name="rglru",
    )(x, vec(gain), w_in, conv_w, vec(conv_b), w_gate, vec(b_r), vec(b_i), vec(lam), w_o)


def kernel(x, l0_ff1_norm, l0_ff1_w_in, l0_ff1_w_out, l0_mix_norm, l0_sb_w_qkv, l0_sb_q_norm, l0_sb_k_norm, l0_sb_w_o, l0_ff2_norm, l0_ff2_w_in, l0_ff2_w_out, l1_ff1_norm, l1_ff1_w_in, l1_ff1_w_out, l1_mix_norm, l1_lru_w_in, l1_lru_conv_w, l1_lru_conv_b, l1_lru_w_r, l1_lru_b_r, l1_lru_w_i, l1_lru_b_i, l1_lru_lambda, l1_lru_w_o, l1_ff2_norm, l1_ff2_w_in, l1_ff2_w_out):
    batch, seq, d = x.shape
    assert d == D_MODEL
    n = batch * seq
    bf = lambda w: w.astype(BF16)
    as_seq = lambda t: t.reshape(batch, seq, D_MODEL)
    x = x.reshape(n, D_MODEL)

    x = _ffn(x, l0_ff1_norm, bf(l0_ff1_w_in), bf(l0_ff1_w_out))
    q, k, v = _qkv(x, l0_mix_norm, bf(l0_sb_w_qkv), l0_sb_q_norm, l0_sb_k_norm)
    o = _stick_breaking(as_seq(q), as_seq(k), as_seq(v), batch=batch, seq=seq)
    x = _ffn(x, l0_ff2_norm, bf(l0_ff2_w_in), bf(l0_ff2_w_out),
             mix=(o.reshape(n, D_MODEL), bf(l0_sb_w_o)))

    x = _ffn(x, l1_ff1_norm, bf(l1_ff1_w_in), bf(l1_ff1_w_out))
    x = _rglru(as_seq(x), l1_mix_norm, bf(l1_lru_w_in), l1_lru_conv_w, l1_lru_conv_b,
               l1_lru_w_r, l1_lru_b_r, l1_lru_w_i, l1_lru_b_i, l1_lru_lambda, bf(l1_lru_w_o))
    x = _ffn(x.reshape(n, D_MODEL), l1_ff2_norm, bf(l1_ff2_w_in), bf(l1_ff2_w_out))
    return as_seq(x)
```

```python
import functools
import math

import jax
import jax.numpy as jnp
from jax import lax
from jax.experimental import pallas as pl
from jax.experimental.pallas import tpu as pltpu

D_MODEL = 1024
SB_HEADS = 16
SB_HEAD_DIM = D_MODEL // SB_HEADS
LRU_BLOCKS = 16
LRU_BLOCK_W = D_MODEL // LRU_BLOCKS
LRU_C = 8.0
CONV_W = 4
NORM_EPS = 1e-6
LOG2_E = math.log2(math.e)

V7X_LANES = 128
V7X_SUBLANES = 8
V7X_MXU_DIM = 256
V7X_VMEM_BYTES = 64 * 1024 * 1024

F32 = jnp.float32
BF16 = jnp.bfloat16

FFN_ROWS = 512
FFN_CHUNK = V7X_MXU_DIM
PROJ_ROWS = 512
SB_QUERIES = 256
SB_KEYS = 128
SB_STEP = SB_QUERIES // SB_KEYS
SB_GROUPS = 4
SB_MASKED_LOGIT = -1e30
SB_DEAD_CARRY = 128.0
LRU_STEPS = 64


def _vmem_limit(resident_bytes):
    return int(min(resident_bytes * 3 // 2 + (4 << 20), V7X_VMEM_BYTES - (4 << 20)))


def _dot(a, b):
    return jnp.dot(a, b, preferred_element_type=F32)


def _rms_norm_rows(x, gain_row):
    ms = jnp.mean(x * x, axis=-1, keepdims=True)
    return x * lax.rsqrt(ms + NORM_EPS) * gain_row


def _split_bf16(v):
    hi = v.astype(BF16)
    lo = (v - hi.astype(F32)).astype(BF16)
    return hi, lo


def _resident(shape):
    zeros = (0,) * len(shape)
    return pl.BlockSpec(shape, lambda *_: zeros, pipeline_mode=pl.Buffered(1))


def _ffn_kernel(x_ref, g_ref, win_ref, wout_ref, *rest, d_ff):
    x = x_ref[...]
    if len(rest) == 3:
        a_ref, wmix_ref, o_ref = rest
        x = x + _dot(a_ref[...], wmix_ref[...].astype(BF16))
    else:
        o_ref, = rest
    xn = _rms_norm_rows(x, g_ref[...]).astype(BF16)
    y = jnp.zeros_like(x)
    for c in range(d_ff // FFN_CHUNK):
        lo = c * FFN_CHUNK
        gate = _dot(xn, win_ref[:, lo:lo + FFN_CHUNK].astype(BF16))
        up = _dot(xn, win_ref[:, d_ff + lo:d_ff + lo + FFN_CHUNK].astype(BF16))
        h = (gate * jax.nn.sigmoid(gate) * up).astype(BF16)
        y = y + _dot(h, wout_ref[lo:lo + FFN_CHUNK, :].astype(BF16))
    o_ref[...] = x + 0.5 * y


def _ffn(x, gain, w_in, w_out, mix=None):
    n = x.shape[0]
    d_ff = w_out.shape[0]
    assert d_ff % FFN_CHUNK == 0 and n % FFN_ROWS == 0
    row = pl.BlockSpec((FFN_ROWS, D_MODEL), lambda i: (i, 0))
    operands = [x, gain.reshape(1, D_MODEL), w_in, w_out]
    in_specs = [row, _resident((1, D_MODEL)), _resident(w_in.shape), _resident(w_out.shape)]
    resident = 4 * (w_in.size + w_out.size) + 4 * 4 * FFN_ROWS * D_MODEL \
        + 4 * FFN_ROWS * (3 * D_MODEL + 3 * FFN_CHUNK)
    if mix is not None:
        a, w_mix = mix
        operands += [a, w_mix]
        in_specs += [row, _resident(w_mix.shape)]
        resident += 4 * w_mix.size + 2 * 2 * FFN_ROWS * D_MODEL
    return pl.pallas_call(
        functools.partial(_ffn_kernel, d_ff=d_ff),
        grid=(n // FFN_ROWS,),
        in_specs=in_specs,
        out_specs=row,
        out_shape=jax.ShapeDtypeStruct((n, D_MODEL), F32),
        compiler_params=pltpu.CompilerParams(
            dimension_semantics=("arbitrary",), vmem_limit_bytes=_vmem_limit(resident)),
        name="ffn",
    )(*operands)


def _head_rms_norm(t, mean_ref, gain_row):
    outs = []
    for c in range(D_MODEL // V7X_MXU_DIM):
        tc = t[:, c * V7X_MXU_DIM:(c + 1) * V7X_MXU_DIM]
        hi, lo = _split_bf16(tc * tc)
        ms = _dot(hi, mean_ref[...]) + _dot(lo, mean_ref[...])
        outs.append(tc * lax.rsqrt(ms + NORM_EPS))
    return jnp.concatenate(outs, axis=-1) * gain_row


def _qkv_kernel(x_ref, g_ref, w_ref, mean_ref, qg_ref, kg_ref, q_ref, k_ref, v_ref):
    xn = _rms_norm_rows(x_ref[...], g_ref[...]).astype(BF16)
    q = _dot(xn, w_ref[:, 0:D_MODEL].astype(BF16))
    q_ref[...] = (_head_rms_norm(q, mean_ref, qg_ref[...]) * (1.0 / math.sqrt(SB_HEAD_DIM))).astype(BF16)
    k = _dot(xn, w_ref[:, D_MODEL:2 * D_MODEL].astype(BF16))
    k_ref[...] = _head_rms_norm(k, mean_ref, kg_ref[...]).astype(BF16)
    v_ref[...] = _dot(xn, w_ref[:, 2 * D_MODEL:3 * D_MODEL].astype(BF16)).astype(BF16)


def _head_mean_matrix():
    r = lax.broadcasted_iota(jnp.int32, (V7X_MXU_DIM, V7X_MXU_DIM), 0) // SB_HEAD_DIM
    c = lax.broadcasted_iota(jnp.int32, (V7X_MXU_DIM, V7X_MXU_DIM), 1) // SB_HEAD_DIM
    return jnp.where(r == c, 1.0 / SB_HEAD_DIM, 0.0).astype(BF16)


def _qkv(x, gain, w_qkv, q_gain, k_gain):
    n = x.shape[0]
    row = pl.BlockSpec((PROJ_ROWS, D_MODEL), lambda i: (i, 0))
    out = jax.ShapeDtypeStruct((n, D_MODEL), BF16)
    resident = 4 * w_qkv.size + 4 * 2 * PROJ_ROWS * D_MODEL + 3 * 2 * 2 * PROJ_ROWS * D_MODEL \
        + 4 * 4 * PROJ_ROWS * D_MODEL
    tile_gain = lambda g: jnp.tile(g, SB_HEADS).reshape(1, D_MODEL)
    return pl.pallas_call(
        _qkv_kernel,
        grid=(n // PROJ_ROWS,),
        in_specs=[row, _resident((1, D_MODEL)), _resident(w_qkv.shape),
                  _resident((V7X_MXU_DIM, V7X_MXU_DIM)),
                  _resident((1, D_MODEL)), _resident((1, D_MODEL))],
        out_specs=[row, row, row],
        out_shape=[out, out, out],
        compiler_params=pltpu.CompilerParams(
            dimension_semantics=("arbitrary",), vmem_limit_bytes=_vmem_limit(resident)),
        name="qkv",
    )(x, gain.reshape(1, D_MODEL), w_qkv, _head_mean_matrix(), tile_gain(q_gain), tile_gain(k_gain))


def _sb_kernel(q_ref, k_ref, v_ref, cs_ref, o_ref, kh_s, vh_s, x_s, total_s, carry_s, acc_s, *, seq):
    Q, K = SB_QUERIES, SB_KEYS
    lanes = [slice(g * V7X_LANES, (g + 1) * V7X_LANES) for g in range(SB_GROUPS)]

    lane = lax.broadcasted_iota(jnp.int32, (K, V7X_LANES), 1)
    head_lanes = [jnp.where((lane < SB_HEAD_DIM) == (h == 0), 1.0, 0.0).astype(BF16) for h in range(2)]

    def head_rows(kb, count=1):
        return pl.ds(pl.multiple_of(kb * 2 * K, 2 * K), count * 2 * K)

    def split_heads(kb, _):
        rows = pl.ds(pl.multiple_of(kb * K, K), K)
        for g in range(SB_GROUPS):
            for src, dst in ((k_ref, kh_s), (v_ref, vh_s)):
                blk = src[rows, lanes[g]]
                dst[g, head_rows(kb), :] = jnp.concatenate(
                    [blk * head_lanes[0], blk * head_lanes[1]], axis=0)
        return 0

    lax.fori_loop(0, seq // K, split_heads, 0)

    def score(qs, q0, kbs, diagonal):
        work = [(g, i, kb) for g in range(SB_GROUPS) for i, kb in enumerate(kbs)]
        first_row = lambda i: (len(kbs) - 1 - i) * K if diagonal else 0
        logits, s_parts = [], []
        for g, i, kb in work:
            r0 = first_row(i)
            z = lax.dot_general(qs[g][r0:], kh_s[g, head_rows(kb), :], (((1,), (1,)), ((), ())),
                                preferred_element_type=F32)
            s = jnp.maximum(z, 0.0) + jnp.log(1.0 + jnp.exp2(jnp.abs(z) * -LOG2_E))
            if diagonal:
                row = lax.broadcasted_iota(jnp.int32, (Q - r0, 2 * K), 0) + r0
                col = lax.broadcasted_iota(jnp.int32, (Q - r0, 2 * K), 1)
                earlier = (kb * K + (col & (K - 1))) < (q0 + row)
                s = jnp.where(earlier, s, 0.0)
                z = jnp.where(earlier, z, SB_MASKED_LOGIT)
            s_hi, s_lo = _split_bf16(s)
            for h in range(2):
                s_parts.append(jnp.concatenate([s_hi[:, h * K:(h + 1) * K],
                                                s_lo[:, h * K:(h + 1) * K]], axis=1))
            logits.append(z)
        sums = _dot(jnp.concatenate(s_parts, axis=0), cs_ref[...])
        offset = 0
        for n, (g, i, kb) in enumerate(work):
            r0 = first_row(i)
            h0 = sums[offset:offset + Q - r0]
            h1 = sums[offset + Q - r0:offset + 2 * (Q - r0)]
            offset += 2 * (Q - r0)
            from_key = jnp.concatenate([h0[:, :K], h1[:, :K]], axis=1)
            total = jnp.concatenate([h0[:, K:], h1[:, K:]], axis=1)
            x_s[g, i, r0:, :] = logits[n] - from_key if i == 0 else logits[n] - from_key - later[r0:]
            if r0:
                x_s[g, i, :r0, :] = jnp.full((r0, 2 * K), SB_MASKED_LOGIT, F32)
                total = jnp.concatenate([jnp.zeros((r0, 2 * K), F32), total], axis=0)
            later = total if i == 0 else later + total
            if i == len(kbs) - 1:
                total_s[g] = later

    def fold(kbs):
        for g in range(SB_GROUPS):
            carry = carry_s[g]
            w = [jnp.exp(x_s[g, i] - carry).astype(BF16) for i in reversed(range(len(kbs)))]
            values = vh_s[g, head_rows(kbs[-1], len(kbs)), :]
            acc_s[g] += _dot(jnp.concatenate(w, axis=1), values)
            carry_s[g] = carry + total_s[g]

    def smallest_carry_past_scored():
        smallest = None
        for g in range(SB_GROUPS):
            carry = carry_s[g] + total_s[g]
            smallest = carry if smallest is None else jnp.minimum(smallest, carry)
        smallest = jnp.minimum(smallest[:, :K], smallest[:, K:])
        smallest = jnp.min(smallest.reshape(Q // V7X_SUBLANES, V7X_SUBLANES, K), axis=0)
        return jnp.min(smallest, axis=0, keepdims=True)[0, 0]

    def q_body(qi, _):
        q0 = pl.multiple_of(qi * Q, Q)
        qs = [q_ref[pl.ds(q0, Q), lanes[g]] for g in range(SB_GROUPS)]
        top = (qi + 1) * SB_STEP - 1
        group = lambda j: [top - j * SB_STEP - n for n in range(SB_STEP)]
        carry_s[...] = jnp.zeros_like(carry_s)
        acc_s[...] = jnp.zeros_like(acc_s)
        score(qs, q0, group(0), True)

        def trip(state):
            j, _ = state
            fold(group(j))
            score(qs, q0, group(j + 1), False)
            return j + 1, smallest_carry_past_scored()

        j, _ = lax.while_loop(lambda state: (state[0] < qi) & (state[1] <= SB_DEAD_CARRY), trip,
                              (jnp.int32(0), smallest_carry_past_scored()))
        fold(group(j))

        for g in range(SB_GROUPS):
            o_ref[pl.ds(q0, Q), lanes[g]] = acc_s[g].astype(o_ref.dtype)
        return 0

    lax.fori_loop(0, seq // Q, q_body, 0)


def _sb_constants():
    r = lax.broadcasted_iota(jnp.int32, (2 * SB_KEYS, 2 * SB_KEYS), 0) % SB_KEYS
    c = lax.broadcasted_iota(jnp.int32, (2 * SB_KEYS, 2 * SB_KEYS), 1)
    return jnp.where((c >= SB_KEYS) | (r >= c), 1.0, 0.0).astype(BF16)


def _stick_breaking(q, k, v, *, batch, seq):
    assert seq % SB_QUERIES == 0 and SB_QUERIES == SB_STEP * SB_KEYS and 2 * SB_KEYS == V7X_MXU_DIM
    cs = _sb_constants()
    width = SB_GROUPS * V7X_LANES
    assert D_MODEL % width == 0
    group = pl.BlockSpec((None, seq, width), lambda b, g: (b, 0, g))
    per_head = pltpu.VMEM((SB_GROUPS, 2 * seq, V7X_LANES), BF16)
    scores = (SB_GROUPS, SB_QUERIES, 2 * SB_KEYS)
    resident = 2 * 4 * 2 * seq * width + 2 * cs.size + 2 * 2 * 2 * seq * width \
        + 4 * 8 * SB_GROUPS * SB_STEP * SB_QUERIES * 2 * SB_KEYS
    return pl.pallas_call(
        functools.partial(_sb_kernel, seq=seq),
        grid=(batch, D_MODEL // width),
        in_specs=[group, group, group, _resident(cs.shape)],
        out_specs=group,
        out_shape=jax.ShapeDtypeStruct((batch, seq, D_MODEL), BF16),
        scratch_shapes=[per_head, per_head,
                        pltpu.VMEM((SB_GROUPS, SB_STEP) + scores[1:], F32),
                        pltpu.VMEM(scores, F32), pltpu.VMEM(scores, F32),
                        pltpu.VMEM((SB_GROUPS, SB_QUERIES, V7X_LANES), F32)],
        compiler_params=pltpu.CompilerParams(
            dimension_semantics=("arbitrary", "arbitrary"),
            vmem_limit_bytes=_vmem_limit(resident)),
        name="stick_breaking",
    )(q, k, v, cs)


def _rglru_kernel(x_ref, g_ref, win_ref, cw_ref, cb_ref, wg_ref, br_ref, bi_ref, lam_ref, wo_ref,
                  o_ref, xt_s, ot_s, xb_s, y_s, a_s, u_s, h_s, state_s, *, batch):
    rows = LRU_STEPS * batch
    halo = CONV_W * batch
    lane_tiles = D_MODEL // V7X_LANES

    @pl.when(pl.program_id(0) == 0)
    def _():
        xb_s[0:halo, :] = jnp.zeros((halo, D_MODEL), F32)
        state_s[...] = jnp.zeros_like(state_s)

    for b in range(batch):
        for l in range(lane_tiles):
            xt_s[l, pl.ds(b, LRU_STEPS, stride=batch), :] = \
                x_ref[b, :, l * V7X_LANES:(l + 1) * V7X_LANES]
    x = jnp.concatenate([xt_s[l] for l in range(lane_tiles)], axis=1)
    xn = _rms_norm_rows(x, g_ref[...]).astype(BF16)
    xb_s[halo:halo + rows, :] = _dot(xn, win_ref[:, 0:D_MODEL].astype(BF16))
    y_s[...] = jax.nn.gelu(_dot(xn, win_ref[:, D_MODEL:2 * D_MODEL].astype(BF16)), approximate=True)

    xc = cb_ref[...] + cw_ref[CONV_W - 1:CONV_W, :] * xb_s[halo:halo + rows, :]
    for j in range(CONV_W - 1):
        off = halo - (CONV_W - 1 - j) * batch
        xc = xc + cw_ref[j:j + 1, :] * xb_s[off:off + rows, :]
    xb_s[0:halo, :] = xb_s[rows:rows + halo, :]

    xc_b = xc.astype(BF16)
    r_parts, i_parts = [], []
    for g in range(D_MODEL // V7X_MXU_DIM):
        ri = _dot(xc_b[:, g * V7X_MXU_DIM:(g + 1) * V7X_MXU_DIM], wg_ref[g])
        r_parts.append(ri[:, :V7X_MXU_DIM])
        i_parts.append(ri[:, V7X_MXU_DIM:])
    r = jax.nn.sigmoid(jnp.concatenate(r_parts, axis=-1) + br_ref[...])
    i = jax.nn.sigmoid(jnp.concatenate(i_parts, axis=-1) + bi_ref[...])
    lam = lam_ref[...]
    log_sig_lam = jnp.minimum(lam, 0.0) - jnp.log1p(jnp.exp(-jnp.abs(lam)))
    log_a = (LRU_C * r) * log_sig_lam
    a_s[...] = jnp.exp(log_a)
    th = jnp.tanh(log_a)
    u_s[...] = jnp.sqrt(-2.0 * th / (1.0 - th)) * (i * xc)

    def step(t, h):
        sl = pl.ds(pl.multiple_of(t * batch, batch), batch)
        h = a_s[sl, :] * h + u_s[sl, :]
        h_s[sl, :] = h
        return h

    state_s[...] = lax.fori_loop(0, LRU_STEPS, step, state_s[...], unroll=8)
    out = x + _dot((h_s[...] * y_s[...]).astype(BF16), wo_ref[...].astype(BF16))
    for l in range(lane_tiles):
        ot_s[l] = out[:, l * V7X_LANES:(l + 1) * V7X_LANES]
    for b in range(batch):
        for l in range(lane_tiles):
            o_ref[b, :, l * V7X_LANES:(l + 1) * V7X_LANES] = \
                ot_s[l, pl.ds(b, LRU_STEPS, stride=batch), :]


def _block_diag_tiles(w):
    per = V7X_MXU_DIM // LRU_BLOCK_W
    w = w.reshape(LRU_BLOCKS // per, per, LRU_BLOCK_W, LRU_BLOCK_W)
    eye = jnp.eye(per, dtype=w.dtype)
    return jnp.einsum("gpcd,pq->gpcqd", w, eye).reshape(LRU_BLOCKS // per, V7X_MXU_DIM, V7X_MXU_DIM)


def _rglru(x, gain, w_in, conv_w, conv_b, w_r, b_r, w_i, b_i, lam, w_o):
    batch, seq, _ = x.shape
    assert batch == V7X_SUBLANES and seq % LRU_STEPS == 0
    rows = LRU_STEPS * batch
    w_gate = jnp.concatenate([_block_diag_tiles(w_r), _block_diag_tiles(w_i)], axis=-1).astype(BF16)
    vec = lambda p: p.reshape(1, D_MODEL)
    chunk = pl.BlockSpec((batch, LRU_STEPS, D_MODEL), lambda i: (0, i, 0))
    slabs = pltpu.VMEM((D_MODEL // V7X_LANES, rows, V7X_LANES), F32)
    resident = 4 * (w_in.size + w_o.size) + 2 * w_gate.size + 4 * 4 * rows * D_MODEL \
        + 4 * 7 * (rows + CONV_W * batch) * D_MODEL + 4 * 6 * rows * D_MODEL
    return pl.pallas_call(
        functools.partial(_rglru_kernel, batch=batch),
        grid=(seq // LRU_STEPS,),
        in_specs=[chunk, _resident((1, D_MODEL)), _resident(w_in.shape),
                  _resident((CONV_W, D_MODEL)), _resident((1, D_MODEL)),
                  _resident(w_gate.shape), _resident((1, D_MODEL)), _resident((1, D_MODEL)),
                  _resident((1, D_MODEL)), _resident(w_o.shape)],
        out_specs=chunk,
        out_shape=jax.ShapeDtypeStruct(x.shape, F32),
        scratch_shapes=[slabs, slabs,
                        pltpu.VMEM((rows + CONV_W * batch, D_MODEL), F32),
                        pltpu.VMEM((rows, D_MODEL), F32),
                        pltpu.VMEM((rows, D_MODEL), F32),
                        pltpu.VMEM((rows, D_MODEL), F32),
                        pltpu.VMEM((rows, D_MODEL), F32),
                        pltpu.VMEM((batch, D_MODEL), F32)],
        compiler_params=pltpu.CompilerParams(
            dimension_semantics=("arbitrary",), vmem_limit_bytes=_vmem_limit(resident)),
        name="rglru",
    )(x, vec(gain), w_in, conv_w, vec(conv_b), w_gate, vec(b_r), vec(b_i), vec(lam), w_o)


def kernel(x, l0_ff1_norm, l0_ff1_w_in, l0_ff1_w_out, l0_mix_norm, l0_sb_w_qkv, l0_sb_q_norm, l0_sb_k_norm, l0_sb_w_o, l0_ff2_norm, l0_ff2_w_in, l0_ff2_w_out, l1_ff1_norm, l1_ff1_w_in, l1_ff1_w_out, l1_mix_norm, l1_lru_w_in, l1_lru_conv_w, l1_lru_conv_b, l1_lru_w_r, l1_lru_b_r, l1_lru_w_i, l1_lru_b_i, l1_lru_lambda, l1_lru_w_o, l1_ff2_norm, l1_ff2_w_in, l1_ff2_w_out):
    batch, seq, d = x.shape
    assert d == D_MODEL
    n = batch * seq
    as_seq = lambda t: t.reshape(batch, seq, D_MODEL)
    x = x.reshape(n, D_MODEL)

    x = _ffn(x, l0_ff1_norm, l0_ff1_w_in, l0_ff1_w_out)
    q, k, v = _qkv(x, l0_mix_norm, l0_sb_w_qkv, l0_sb_q_norm, l0_sb_k_norm)
    o = _stick_breaking(as_seq(q), as_seq(k), as_seq(v), batch=batch, seq=seq)
    x = _ffn(x, l0_ff2_norm, l0_ff2_w_in, l0_ff2_w_out, mix=(o.reshape(n, D_MODEL), l0_sb_w_o))

    x = _ffn(x, l1_ff1_norm, l1_ff1_w_in, l1_ff1_w_out)
    x = _rglru(as_seq(x), l1_mix_norm, l1_lru_w_in, l1_lru_conv_w, l1_lru_conv_b,
               l1_lru_w_r, l1_lru_b_r, l1_lru_w_i, l1_lru_b_i, l1_lru_lambda, l1_lru_w_o)
    x = _ffn(x.reshape(n, D_MODEL), l1_ff2_norm, l1_ff2_w_in, l1_ff2_w_out)
    return as_seq(x)
```

```python
import functools
import math

import jax
import jax.numpy as jnp
from jax import lax
from jax.experimental import pallas as pl
from jax.experimental.pallas import tpu as pltpu

D_MODEL = 1024
SB_HEADS = 16
SB_HEAD_DIM = D_MODEL // SB_HEADS
LRU_BLOCKS = 16
LRU_BLOCK_W = D_MODEL // LRU_BLOCKS
LRU_C = 8.0
CONV_W = 4
NORM_EPS = 1e-6
LOG2_E = math.log2(math.e)

V7X_LANES = 128
V7X_SUBLANES = 8
V7X_MXU_DIM = 256
V7X_VMEM_BYTES = 64 * 1024 * 1024

F32 = jnp.float32
BF16 = jnp.bfloat16

FFN_ROWS = 512
FFN_CHUNK = V7X_MXU_DIM
PROJ_ROWS = 512
SB_QUERIES = 256
SB_KEYS = 128
SB_STEP = SB_QUERIES // SB_KEYS
SB_GROUPS = 4
SB_MASKED_LOGIT = -1e30
SB_DEAD_CARRY = 128.0
LRU_STEPS = 64


def _vmem_limit(resident_bytes):
    return int(min(resident_bytes * 3 // 2 + (4 << 20), V7X_VMEM_BYTES - (4 << 20)))


def _dot(a, b):
    return jnp.dot(a, b, preferred_element_type=F32)


def _rms_norm_rows(x, gain_row):
    ms = jnp.mean(x * x, axis=-1, keepdims=True)
    return x * lax.rsqrt(ms + NORM_EPS) * gain_row


def _split_bf16(v):
    hi = v.astype(BF16)
    lo = (v - hi.astype(F32)).astype(BF16)
    return hi, lo


def _resident(shape):
    zeros = (0,) * len(shape)
    return pl.BlockSpec(shape, lambda *_: zeros, pipeline_mode=pl.Buffered(1))


def _ffn_kernel(x_ref, g_ref, win_ref, wout_ref, *rest, d_ff):
    x = x_ref[...]
    if len(rest) == 3:
        a_ref, wmix_ref, o_ref = rest
        x = x + _dot(a_ref[...], wmix_ref[...].astype(BF16))
    else:
        o_ref, = rest
    xn = _rms_norm_rows(x, g_ref[...]).astype(BF16)
    y = jnp.zeros_like(x)
    for c in range(d_ff // FFN_CHUNK):
        lo = c * FFN_CHUNK
        gate = _dot(xn, win_ref[:, lo:lo + FFN_CHUNK].astype(BF16))
        up = _dot(xn, win_ref[:, d_ff + lo:d_ff + lo + FFN_CHUNK].astype(BF16))
        h = (gate * jax.nn.sigmoid(gate) * up).astype(BF16)
        y = y + _dot(h, wout_ref[lo:lo + FFN_CHUNK, :].astype(BF16))
    o_ref[...] = x + 0.5 * y


def _ffn(x, gain, w_in, w_out, mix=None):
    n = x.shape[0]
    d_ff = w_out.shape[0]
    assert d_ff % FFN_CHUNK == 0 and n % FFN_ROWS == 0
    row = pl.BlockSpec((FFN_ROWS, D_MODEL), lambda i: (i, 0))
    operands = [x, gain.reshape(1, D_MODEL), w_in, w_out]
    in_specs = [row, _resident((1, D_MODEL)), _resident(w_in.shape), _resident(w_out.shape)]
    resident = 4 * (w_in.size + w_out.size) + 4 * 4 * FFN_ROWS * D_MODEL \
        + 4 * FFN_ROWS * (3 * D_MODEL + 3 * FFN_CHUNK)
    if mix is not None:
        a, w_mix = mix
        operands += [a, w_mix]
        in_specs += [row, _resident(w_mix.shape)]
        resident += 4 * w_mix.size + 2 * 2 * FFN_ROWS * D_MODEL
    return pl.pallas_call(
        functools.partial(_ffn_kernel, d_ff=d_ff),
        grid=(n // FFN_ROWS,),
        in_specs=in_specs,
        out_specs=row,
        out_shape=jax.ShapeDtypeStruct((n, D_MODEL), F32),
        compiler_params=pltpu.CompilerParams(
            dimension_semantics=("arbitrary",), vmem_limit_bytes=_vmem_limit(resident)),
        name="ffn",
    )(*operands)


def _head_rms_norm(t, mean_ref, gain_row):
    outs = []
    for c in range(D_MODEL // V7X_MXU_DIM):
        tc = t[:, c * V7X_MXU_DIM:(c + 1) * V7X_MXU_DIM]
        hi, lo = _split_bf16(tc * tc)
        ms = _dot(hi, mean_ref[...]) + _dot(lo, mean_ref[...])
        outs.append(tc * lax.rsqrt(ms + NORM_EPS))
    return jnp.concatenate(outs, axis=-1) * gain_row


def _qkv_kernel(x_ref, g_ref, w_ref, mean_ref, qg_ref, kg_ref, q_ref, k_ref, v_ref):
    xn = _rms_norm_rows(x_ref[...], g_ref[...]).astype(BF16)
    q = _dot(xn, w_ref[:, 0:D_MODEL].astype(BF16))
    q_ref[...] = (_head_rms_norm(q, mean_ref, qg_ref[...]) * (1.0 / math.sqrt(SB_HEAD_DIM))).astype(BF16)
    k = _dot(xn, w_ref[:, D_MODEL:2 * D_MODEL].astype(BF16))
    k_ref[...] = _head_rms_norm(k, mean_ref, kg_ref[...]).astype(BF16)
    v_ref[...] = _dot(xn, w_ref[:, 2 * D_MODEL:3 * D_MODEL].astype(BF16)).astype(BF16)


def _head_mean_matrix():
    r = lax.broadcasted_iota(jnp.int32, (V7X_MXU_DIM, V7X_MXU_DIM), 0) // SB_HEAD_DIM
    c = lax.broadcasted_iota(jnp.int32, (V7X_MXU_DIM, V7X_MXU_DIM), 1) // SB_HEAD_DIM
    return jnp.where(r == c, 1.0 / SB_HEAD_DIM, 0.0).astype(BF16)


def _qkv(x, gain, w_qkv, q_gain, k_gain):
    n = x.shape[0]
    row = pl.BlockSpec((PROJ_ROWS, D_MODEL), lambda i: (i, 0))
    out = jax.ShapeDtypeStruct((n, D_MODEL), BF16)
    resident = 4 * w_qkv.size + 4 * 2 * PROJ_ROWS * D_MODEL + 3 * 2 * 2 * PROJ_ROWS * D_MODEL \
        + 4 * 4 * PROJ_ROWS * D_MODEL
    tile_gain = lambda g: jnp.tile(g, SB_HEADS).reshape(1, D_MODEL)
    return pl.pallas_call(
        _qkv_kernel,
        grid=(n // PROJ_ROWS,),
        in_specs=[row, _resident((1, D_MODEL)), _resident(w_qkv.shape),
                  _resident((V7X_MXU_DIM, V7X_MXU_DIM)),
                  _resident((1, D_MODEL)), _resident((1, D_MODEL))],
        out_specs=[row, row, row],
        out_shape=[out, out, out],
        compiler_params=pltpu.CompilerParams(
            dimension_semantics=("arbitrary",), vmem_limit_bytes=_vmem_limit(resident)),
        name="qkv",
    )(x, gain.reshape(1, D_MODEL), w_qkv, _head_mean_matrix(), tile_gain(q_gain), tile_gain(k_gain))


def _sb_kernel(q_ref, k_ref, v_ref, cs_ref, o_ref, kh_s, vh_s, x_s, total_s, carry_s, acc_s, *, seq):
    Q, K = SB_QUERIES, SB_KEYS
    lanes = [slice(g * V7X_LANES, (g + 1) * V7X_LANES) for g in range(SB_GROUPS)]

    lane = lax.broadcasted_iota(jnp.int32, (K, V7X_LANES), 1)
    head_lanes = [jnp.where((lane < SB_HEAD_DIM) == (h == 0), 1.0, 0.0).astype(BF16) for h in range(2)]

    def head_rows(kb, count=1):
        return pl.ds(pl.multiple_of(kb * 2 * K, 2 * K), count * 2 * K)

    def split_heads(kb, _):
        rows = pl.ds(pl.multiple_of(kb * K, K), K)
        for g in range(SB_GROUPS):
            for src, dst in ((k_ref, kh_s), (v_ref, vh_s)):
                blk = src[rows, lanes[g]]
                dst[g, head_rows(kb), :] = jnp.concatenate(
                    [blk * head_lanes[0], blk * head_lanes[1]], axis=0)
        return 0

    lax.fori_loop(0, seq // K, split_heads, 0)

    def score(qs, q0, kbs, diagonal):
        work = [(g, i, kb) for g in range(SB_GROUPS) for i, kb in enumerate(kbs)]
        first_row = lambda i: (len(kbs) - 1 - i) * K if diagonal else 0
        logits, s_parts = [], []
        for g, i, kb in work:
            r0 = first_row(i)
            z = lax.dot_general(qs[g][r0:], kh_s[g, head_rows(kb), :], (((1,), (1,)), ((), ())),
                                preferred_element_type=F32)
            s = jnp.maximum(z, 0.0) + jnp.log(1.0 + jnp.exp2(jnp.abs(z) * -LOG2_E))
            if diagonal:
                row = lax.broadcasted_iota(jnp.int32, (Q - r0, 2 * K), 0) + r0
                col = lax.broadcasted_iota(jnp.int32, (Q - r0, 2 * K), 1)
                earlier = (kb * K + (col & (K - 1))) < (q0 + row)
                s = jnp.where(earlier, s, 0.0)
                z = jnp.where(earlier, z, SB_MASKED_LOGIT)
            s_hi, s_lo = _split_bf16(s)
            for h in range(2):
                s_parts.append(jnp.concatenate([s_hi[:, h * K:(h + 1) * K],
                                                s_lo[:, h * K:(h + 1) * K]], axis=1))
            logits.append(z)
        sums = _dot(jnp.concatenate(s_parts, axis=0), cs_ref[...])
        offset = 0
        for n, (g, i, kb) in enumerate(work):
            r0 = first_row(i)
            h0 = sums[offset:offset + Q - r0]
            h1 = sums[offset + Q - r0:offset + 2 * (Q - r0)]
            offset += 2 * (Q - r0)
            from_key = jnp.concatenate([h0[:, :K], h1[:, :K]], axis=1)
            total = jnp.concatenate([h0[:, K:], h1[:, K:]], axis=1)
            x_s[g, i, r0:, :] = logits[n] - from_key if i == 0 else logits[n] - from_key - later[r0:]
            if r0:
                x_s[g, i, :r0, :] = jnp.full((r0, 2 * K), SB_MASKED_LOGIT, F32)
                total = jnp.concatenate([jnp.zeros((r0, 2 * K), F32), total], axis=0)
            later = total if i == 0 else later + total
            if i == len(kbs) - 1:
                total_s[g] = later

    def fold(kbs):
        for g in range(SB_GROUPS):
            carry = carry_s[g]
            w = [jnp.exp(x_s[g, i] - carry).astype(BF16) for i in reversed(range(len(kbs)))]
            values = vh_s[g, head_rows(kbs[-1], len(kbs)), :]
            acc_s[g] += _dot(jnp.concatenate(w, axis=1), values)
            carry_s[g] = carry + total_s[g]

    def smallest_carry_past_scored():
        smallest = None
        for g in range(SB_GROUPS):
            carry = carry_s[g] + total_s[g]
            smallest = carry if smallest is None else jnp.minimum(smallest, carry)
        smallest = jnp.minimum(smallest[:, :K], smallest[:, K:])
        smallest = jnp.min(smallest.reshape(Q // V7X_SUBLANES, V7X_SUBLANES, K), axis=0)
        return jnp.min(smallest, axis=0, keepdims=True)[0, 0]

    def q_block(qi):
        q0 = pl.multiple_of(qi * Q, Q)
        qs = [q_ref[pl.ds(q0, Q), lanes[g]] for g in range(SB_GROUPS)]
        top = (qi + 1) * SB_STEP - 1
        group = lambda j: [top - j * SB_STEP - n for n in range(SB_STEP)]
        carry_s[...] = jnp.zeros_like(carry_s)
        acc_s[...] = jnp.zeros_like(acc_s)
        score(qs, q0, group(0), True)

        def trip(state):
            j, _ = state
            fold(group(j))
            score(qs, q0, group(j + 1), False)
            return j + 1, smallest_carry_past_scored()

        last = 0
        if not isinstance(qi, int):
            done, smallest = trip((0, None))
            last, _ = lax.while_loop(
                lambda state: (state[0] < qi) & (state[1] <= SB_DEAD_CARRY), trip,
                (jnp.int32(done), smallest))
        fold(group(last))

        for g in range(SB_GROUPS):
            o_ref[pl.ds(q0, Q), lanes[g]] = acc_s[g].astype(o_ref.dtype)

    q_block(0)
    lax.fori_loop(1, seq // Q, lambda qi, _: q_block(qi) or 0, 0)


def _sb_constants():
    r = lax.broadcasted_iota(jnp.int32, (2 * SB_KEYS, 2 * SB_KEYS), 0) % SB_KEYS
    c = lax.broadcasted_iota(jnp.int32, (2 * SB_KEYS, 2 * SB_KEYS), 1)
    return jnp.where((c >= SB_KEYS) | (r >= c), 1.0, 0.0).astype(BF16)


def _stick_breaking(q, k, v, *, batch, seq):
    assert seq % SB_QUERIES == 0 and SB_QUERIES == SB_STEP * SB_KEYS and 2 * SB_KEYS == V7X_MXU_DIM
    cs = _sb_constants()
    width = SB_GROUPS * V7X_LANES
    assert D_MODEL % width == 0
    group = pl.BlockSpec((None, seq, width), lambda b, g: (b, 0, g))
    per_head = pltpu.VMEM((SB_GROUPS, 2 * seq, V7X_LANES), BF16)
    scores = (SB_GROUPS, SB_QUERIES, 2 * SB_KEYS)
    resident = 2 * 4 * 2 * seq * width + 2 * cs.size + 2 * 2 * 2 * seq * width \
        + 4 * 8 * SB_GROUPS * SB_STEP * SB_QUERIES * 2 * SB_KEYS
    return pl.pallas_call(
        functools.partial(_sb_kernel, seq=seq),
        grid=(batch, D_MODEL // width),
        in_specs=[group, group, group, _resident(cs.shape)],
        out_specs=group,
        out_shape=jax.ShapeDtypeStruct((batch, seq, D_MODEL), BF16),
        scratch_shapes=[per_head, per_head,
                        pltpu.VMEM((SB_GROUPS, SB_STEP) + scores[1:], F32),
                        pltpu.VMEM(scores, F32), pltpu.VMEM(scores, F32),
                        pltpu.VMEM((SB_GROUPS, SB_QUERIES, V7X_LANES), F32)],
        compiler_params=pltpu.CompilerParams(
            dimension_semantics=("arbitrary", "arbitrary"),
            vmem_limit_bytes=_vmem_limit(resident)),
        name="stick_breaking",
    )(q, k, v, cs)


def _rglru_kernel(x_ref, g_ref, win_ref, cw_ref, cb_ref, wg_ref, br_ref, bi_ref, lam_ref, wo_ref,
                  o_ref, xt_s, ot_s, xb_s, y_s, ri_s, a_s, u_s, h_s, state_s, *, batch):
    rows = LRU_STEPS * batch
    halo = CONV_W * batch
    lane_tiles = D_MODEL // V7X_LANES

    @pl.when(pl.program_id(0) == 0)
    def _():
        xb_s[0:halo, :] = jnp.zeros((halo, D_MODEL), F32)
        state_s[...] = jnp.zeros_like(state_s)

    for b in range(batch):
        for l in range(lane_tiles):
            xt_s[l, pl.ds(b, LRU_STEPS, stride=batch), :] = \
                x_ref[b, :, l * V7X_LANES:(l + 1) * V7X_LANES]
    x = jnp.concatenate([xt_s[l] for l in range(lane_tiles)], axis=1)
    xn = _rms_norm_rows(x, g_ref[...]).astype(BF16)
    lam = lam_ref[...]
    log_sig_lam = jnp.minimum(lam, 0.0) - jnp.log1p(jnp.exp(-jnp.abs(lam)))

    for g in range(D_MODEL // V7X_MXU_DIM):
        cols = slice(g * V7X_MXU_DIM, (g + 1) * V7X_MXU_DIM)
        xb_s[halo:halo + rows, cols] = _dot(xn, win_ref[:, cols].astype(BF16))
        y_s[:, cols] = _dot(xn, win_ref[:, D_MODEL + cols.start:D_MODEL + cols.stop].astype(BF16))
        xc = cb_ref[:, cols] + cw_ref[CONV_W - 1:CONV_W, cols] * xb_s[halo:halo + rows, cols]
        for j in range(CONV_W - 1):
            off = halo - (CONV_W - 1 - j) * batch
            xc = xc + cw_ref[j:j + 1, cols] * xb_s[off:off + rows, cols]
        u_s[:, cols] = xc
        ri_s[g] = _dot(xc.astype(BF16), wg_ref[g])
        r = jax.nn.sigmoid(ri_s[g, :, :V7X_MXU_DIM] + br_ref[:, cols])
        i = jax.nn.sigmoid(ri_s[g, :, V7X_MXU_DIM:] + bi_ref[:, cols])
        log_a = (LRU_C * r) * log_sig_lam[:, cols]
        a_s[:, cols] = jnp.exp(log_a)
        th = jnp.tanh(log_a)
        u_s[:, cols] = jnp.sqrt(-2.0 * th / (1.0 - th)) * (i * u_s[:, cols])
    xb_s[0:halo, :] = xb_s[rows:rows + halo, :]

    def step(t, h):
        sl = pl.ds(pl.multiple_of(t * batch, batch), batch)
        h = a_s[sl, :] * h + u_s[sl, :]
        h_s[sl, :] = h
        return h

    state_s[...] = lax.fori_loop(0, LRU_STEPS, step, state_s[...], unroll=8)
    y = jax.nn.gelu(y_s[...], approximate=True)
    out = x + _dot((h_s[...] * y).astype(BF16), wo_ref[...].astype(BF16))
    for l in range(lane_tiles):
        ot_s[l] = out[:, l * V7X_LANES:(l + 1) * V7X_LANES]
    for b in range(batch):
        for l in range(lane_tiles):
            o_ref[b, :, l * V7X_LANES:(l + 1) * V7X_LANES] = \
                ot_s[l, pl.ds(b, LRU_STEPS, stride=batch), :]


def _block_diag_tiles(w):
    per = V7X_MXU_DIM // LRU_BLOCK_W
    w = w.reshape(LRU_BLOCKS // per, per, LRU_BLOCK_W, LRU_BLOCK_W)
    eye = jnp.eye(per, dtype=w.dtype)
    return jnp.einsum("gpcd,pq->gpcqd", w, eye).reshape(LRU_BLOCKS // per, V7X_MXU_DIM, V7X_MXU_DIM)


def _rglru(x, gain, w_in, conv_w, conv_b, w_r, b_r, w_i, b_i, lam, w_o):
    batch, seq, _ = x.shape
    assert batch == V7X_SUBLANES and seq % LRU_STEPS == 0
    rows = LRU_STEPS * batch
    w_gate = jnp.concatenate([_block_diag_tiles(w_r), _block_diag_tiles(w_i)], axis=-1).astype(BF16)
    vec = lambda p: p.reshape(1, D_MODEL)
    chunk = pl.BlockSpec((batch, LRU_STEPS, D_MODEL), lambda i: (0, i, 0))
    slabs = pltpu.VMEM((D_MODEL // V7X_LANES, rows, V7X_LANES), F32)
    resident = 4 * (w_in.size + w_o.size) + 2 * w_gate.size + 4 * 4 * rows * D_MODEL \
        + 4 * 9 * (rows + CONV_W * batch) * D_MODEL + 4 * 6 * rows * D_MODEL
    return pl.pallas_call(
        functools.partial(_rglru_kernel, batch=batch),
        grid=(seq // LRU_STEPS,),
        in_specs=[chunk, _resident((1, D_MODEL)), _resident(w_in.shape),
                  _resident((CONV_W, D_MODEL)), _resident((1, D_MODEL)),
                  _resident(w_gate.shape), _resident((1, D_MODEL)), _resident((1, D_MODEL)),
                  _resident((1, D_MODEL)), _resident(w_o.shape)],
        out_specs=chunk,
        out_shape=jax.ShapeDtypeStruct(x.shape, F32),
        scratch_shapes=[slabs, slabs,
                        pltpu.VMEM((rows + CONV_W * batch, D_MODEL), F32),
                        pltpu.VMEM((rows, D_MODEL), F32),
                        pltpu.VMEM((D_MODEL // V7X_MXU_DIM, rows, 2 * V7X_MXU_DIM), F32),
                        pltpu.VMEM((rows, D_MODEL), F32),
                        pltpu.VMEM((rows, D_MODEL), F32),
                        pltpu.VMEM((rows, D_MODEL), F32),
                        pltpu.VMEM((batch, D_MODEL), F32)],
        compiler_params=pltpu.CompilerParams(
            dimension_semantics=("arbitrary",), vmem_limit_bytes=_vmem_limit(resident)),
        name="rglru",
    )(x, vec(gain), w_in, conv_w, vec(conv_b), w_gate, vec(b_r), vec(b_i), vec(lam), w_o)


def kernel(x, l0_ff1_norm, l0_ff1_w_in, l0_ff1_w_out, l0_mix_norm, l0_sb_w_qkv, l0_sb_q_norm, l0_sb_k_norm, l0_sb_w_o, l0_ff2_norm, l0_ff2_w_in, l0_ff2_w_out, l1_ff1_norm, l1_ff1_w_in, l1_ff1_w_out, l1_mix_norm, l1_lru_w_in, l1_lru_conv_w, l1_lru_conv_b, l1_lru_w_r, l1_lru_b_r, l1_lru_w_i, l1_lru_b_i, l1_lru_lambda, l1_lru_w_o, l1_ff2_norm, l1_ff2_w_in, l1_ff2_w_out):
    batch, seq, d = x.shape
    assert d == D_MODEL
    n = batch * seq
    as_seq = lambda t: t.reshape(batch, seq, D_MODEL)
    x = x.reshape(n, D_MODEL)

    x = _ffn(x, l0_ff1_norm, l0_ff1_w_in, l0_ff1_w_out)
    q, k, v = _qkv(x, l0_mix_norm, l0_sb_w_qkv, l0_sb_q_norm, l0_sb_k_norm)
    o = _stick_breaking(as_seq(q), as_seq(k), as_seq(v), batch=batch, seq=seq)
    x = _ffn(x, l0_ff2_norm, l0_ff2_w_in, l0_ff2_w_out, mix=(o.reshape(n, D_MODEL), l0_sb_w_o))

    x = _ffn(x, l1_ff1_norm, l1_ff1_w_in, l1_ff1_w_out)
    x = _rglru(as_seq(x), l1_mix_norm, l1_lru_w_in, l1_lru_conv_w, l1_lru_conv_b,
               l1_lru_w_r, l1_lru_b_r, l1_lru_w_i, l1_lru_b_i, l1_lru_lambda, l1_lru_w_o)
    x = _ffn(x.reshape(n, D_MODEL), l1_ff2_norm, l1_ff2_w_in, l1_ff2_w_out)
    return as_seq(x)
```

```python
import functools
import math

import jax
import jax.numpy as jnp
from jax import lax
from jax.experimental import pallas as pl
from jax.experimental.pallas import tpu as pltpu

D_MODEL = 1024
SB_HEADS = 16
SB_HEAD_DIM = D_MODEL // SB_HEADS
LRU_BLOCKS = 16
LRU_BLOCK_W = D_MODEL // LRU_BLOCKS
LRU_C = 8.0
CONV_W = 4
NORM_EPS = 1e-6
LOG2_E = math.log2(math.e)

V7X_LANES = 128
V7X_SUBLANES = 8
V7X_MXU_DIM = 256
V7X_VMEM_BYTES = 64 * 1024 * 1024

F32 = jnp.float32
BF16 = jnp.bfloat16

FFN_ROWS = 512
FFN_CHUNK = V7X_MXU_DIM
PROJ_ROWS = 1024
SB_QUERIES = 256
SB_KEYS = 128
SB_STEP = SB_QUERIES // SB_KEYS
SB_GROUPS = 4
SB_MASKED_LOGIT = -1e30
SB_DEAD_CARRY = 128.0
LRU_STEPS = 64


def _vmem_limit(resident_bytes):
    return int(min(resident_bytes * 3 // 2 + (4 << 20), V7X_VMEM_BYTES - (4 << 20)))


def _dot(a, b):
    return jnp.dot(a, b, preferred_element_type=F32)


def _rms_norm_rows(x, gain_row):
    ms = jnp.mean(x * x, axis=-1, keepdims=True)
    return x * lax.rsqrt(ms + NORM_EPS) * gain_row


def _split_bf16(v):
    hi = v.astype(BF16)
    lo = (v - hi.astype(F32)).astype(BF16)
    return hi, lo


def _resident(shape):
    zeros = (0,) * len(shape)
    return pl.BlockSpec(shape, lambda *_: zeros, pipeline_mode=pl.Buffered(1))


def _ffn_kernel(x_ref, g_ref, win_ref, wout_ref, *rest, d_ff):
    x = x_ref[...]
    if len(rest) == 3:
        a_ref, wmix_ref, o_ref = rest
        x = x + _dot(a_ref[...], wmix_ref[...].astype(BF16))
    else:
        o_ref, = rest
    xn = _rms_norm_rows(x, g_ref[...]).astype(BF16)
    y = jnp.zeros_like(x)
    for c in range(d_ff // FFN_CHUNK):
        lo = c * FFN_CHUNK
        gate = _dot(xn, win_ref[:, lo:lo + FFN_CHUNK].astype(BF16))
        up = _dot(xn, win_ref[:, d_ff + lo:d_ff + lo + FFN_CHUNK].astype(BF16))
        h = (gate * jax.nn.sigmoid(gate) * up).astype(BF16)
        y = y + _dot(h, wout_ref[lo:lo + FFN_CHUNK, :].astype(BF16))
    o_ref[...] = x + 0.5 * y


def _ffn(x, gain, w_in, w_out, mix=None):
    n = x.shape[0]
    d_ff = w_out.shape[0]
    assert d_ff % FFN_CHUNK == 0 and n % FFN_ROWS == 0
    row = pl.BlockSpec((FFN_ROWS, D_MODEL), lambda i: (i, 0))
    operands = [x, gain.reshape(1, D_MODEL), w_in, w_out]
    in_specs = [row, _resident((1, D_MODEL)), _resident(w_in.shape), _resident(w_out.shape)]
    resident = 4 * (w_in.size + w_out.size) + 4 * 4 * FFN_ROWS * D_MODEL \
        + 4 * FFN_ROWS * (3 * D_MODEL + 3 * FFN_CHUNK)
    if mix is not None:
        a, w_mix = mix
        operands += [a, w_mix]
        in_specs += [row, _resident(w_mix.shape)]
        resident += 4 * w_mix.size + 2 * 2 * FFN_ROWS * D_MODEL
    return pl.pallas_call(
        functools.partial(_ffn_kernel, d_ff=d_ff),
        grid=(n // FFN_ROWS,),
        in_specs=in_specs,
        out_specs=row,
        out_shape=jax.ShapeDtypeStruct((n, D_MODEL), F32),
        compiler_params=pltpu.CompilerParams(
            dimension_semantics=("arbitrary",), vmem_limit_bytes=_vmem_limit(resident)),
        name="ffn",
    )(*operands)


def _head_rms_norm(t, mean_ref, gain_row):
    outs = []
    for c in range(D_MODEL // V7X_MXU_DIM):
        tc = t[:, c * V7X_MXU_DIM:(c + 1) * V7X_MXU_DIM]
        hi, lo = _split_bf16(tc * tc)
        ms = _dot(hi, mean_ref[...]) + _dot(lo, mean_ref[...])
        outs.append(tc * lax.rsqrt(ms + NORM_EPS))
    return jnp.concatenate(outs, axis=-1) * gain_row


def _qkv_kernel(x_ref, g_ref, w_ref, mean_ref, qg_ref, kg_ref, q_ref, k_ref, v_ref):
    xn = _rms_norm_rows(x_ref[...], g_ref[...]).astype(BF16)
    q = _dot(xn, w_ref[:, 0:D_MODEL].astype(BF16))
    q_ref[...] = (_head_rms_norm(q, mean_ref, qg_ref[...]) * (1.0 / math.sqrt(SB_HEAD_DIM))).astype(BF16)
    k = _dot(xn, w_ref[:, D_MODEL:2 * D_MODEL].astype(BF16))
    k_ref[...] = _head_rms_norm(k, mean_ref, kg_ref[...]).astype(BF16)
    v_ref[...] = _dot(xn, w_ref[:, 2 * D_MODEL:3 * D_MODEL].astype(BF16)).astype(BF16)


def _head_mean_matrix():
    r = lax.broadcasted_iota(jnp.int32, (V7X_MXU_DIM, V7X_MXU_DIM), 0) // SB_HEAD_DIM
    c = lax.broadcasted_iota(jnp.int32, (V7X_MXU_DIM, V7X_MXU_DIM), 1) // SB_HEAD_DIM
    return jnp.where(r == c, 1.0 / SB_HEAD_DIM, 0.0).astype(BF16)


def _qkv(x, gain, w_qkv, q_gain, k_gain):
    n = x.shape[0]
    row = pl.BlockSpec((PROJ_ROWS, D_MODEL), lambda i: (i, 0))
    out = jax.ShapeDtypeStruct((n, D_MODEL), BF16)
    resident = 4 * w_qkv.size + 4 * 2 * PROJ_ROWS * D_MODEL + 3 * 2 * 2 * PROJ_ROWS * D_MODEL \
        + 4 * 4 * PROJ_ROWS * D_MODEL
    tile_gain = lambda g: jnp.tile(g, SB_HEADS).reshape(1, D_MODEL)
    return pl.pallas_call(
        _qkv_kernel,
        grid=(n // PROJ_ROWS,),
        in_specs=[row, _resident((1, D_MODEL)), _resident(w_qkv.shape),
                  _resident((V7X_MXU_DIM, V7X_MXU_DIM)),
                  _resident((1, D_MODEL)), _resident((1, D_MODEL))],
        out_specs=[row, row, row],
        out_shape=[out, out, out],
        compiler_params=pltpu.CompilerParams(
            dimension_semantics=("arbitrary",), vmem_limit_bytes=_vmem_limit(resident)),
        name="qkv",
    )(x, gain.reshape(1, D_MODEL), w_qkv, _head_mean_matrix(), tile_gain(q_gain), tile_gain(k_gain))


def _sb_kernel(q_ref, k_ref, v_ref, cs_ref, o_ref, kh_s, vh_s, x_s, total_s, carry_s, acc_s, *, seq):
    Q, K = SB_QUERIES, SB_KEYS
    lanes = [slice(g * V7X_LANES, (g + 1) * V7X_LANES) for g in range(SB_GROUPS)]

    lane = lax.broadcasted_iota(jnp.int32, (K, V7X_LANES), 1)
    head_lanes = [jnp.where((lane < SB_HEAD_DIM) == (h == 0), 1.0, 0.0).astype(BF16) for h in range(2)]

    def head_rows(kb, count=1):
        return pl.ds(pl.multiple_of(kb * 2 * K, 2 * K), count * 2 * K)

    def split_heads(kb, _):
        rows = pl.ds(pl.multiple_of(kb * K, K), K)
        for g in range(SB_GROUPS):
            for src, dst in ((k_ref, kh_s), (v_ref, vh_s)):
                blk = src[rows, lanes[g]]
                dst[g, head_rows(kb), :] = jnp.concatenate(
                    [blk * head_lanes[0], blk * head_lanes[1]], axis=0)
        return 0

    lax.fori_loop(0, seq // K, split_heads, 0)

    def score(qs, q0, kbs, diagonal):
        work = [(g, i, kb) for g in range(SB_GROUPS) for i, kb in enumerate(kbs)]
        first_row = lambda i: (len(kbs) - 1 - i) * K if diagonal else 0
        logits, s_parts = [], []
        for g, i, kb in work:
            r0 = first_row(i)
            z = lax.dot_general(qs[g][r0:], kh_s[g, head_rows(kb), :], (((1,), (1,)), ((), ())),
                                preferred_element_type=F32)
            s = jnp.maximum(z, 0.0) + jnp.log(1.0 + jnp.exp2(jnp.abs(z) * -LOG2_E))
            if diagonal:
                row = lax.broadcasted_iota(jnp.int32, (Q - r0, 2 * K), 0) + r0
                col = lax.broadcasted_iota(jnp.int32, (Q - r0, 2 * K), 1)
                earlier = (kb * K + (col & (K - 1))) < (q0 + row)
                s = jnp.where(earlier, s, 0.0)
                z = jnp.where(earlier, z, SB_MASKED_LOGIT)
            s_hi, s_lo = _split_bf16(s)
            for h in range(2):
                s_parts.append(jnp.concatenate([s_hi[:, h * K:(h + 1) * K],
                                                s_lo[:, h * K:(h + 1) * K]], axis=1))
            logits.append(z)
        sums = _dot(jnp.concatenate(s_parts, axis=0), cs_ref[...])
        offset = 0
        for n, (g, i, kb) in enumerate(work):
            r0 = first_row(i)
            h0 = sums[offset:offset + Q - r0]
            h1 = sums[offset + Q - r0:offset + 2 * (Q - r0)]
            offset += 2 * (Q - r0)
            from_key = jnp.concatenate([h0[:, :K], h1[:, :K]], axis=1)
            total = jnp.concatenate([h0[:, K:], h1[:, K:]], axis=1)
            x_s[g, i, r0:, :] = logits[n] - from_key if i == 0 else logits[n] - from_key - later[r0:]
            if r0:
                x_s[g, i, :r0, :] = jnp.full((r0, 2 * K), SB_MASKED_LOGIT, F32)
                total = jnp.concatenate([jnp.zeros((r0, 2 * K), F32), total], axis=0)
            later = total if i == 0 else later + total
            if i == len(kbs) - 1:
                total_s[g] = later

    def fold(kbs):
        for g in range(SB_GROUPS):
            carry = carry_s[g]
            w = [jnp.exp(x_s[g, i] - carry).astype(BF16) for i in reversed(range(len(kbs)))]
            values = vh_s[g, head_rows(kbs[-1], len(kbs)), :]
            acc_s[g] += _dot(jnp.concatenate(w, axis=1), values)
            carry_s[g] = carry + total_s[g]

    def smallest_carry_past_scored():
        smallest = None
        for g in range(SB_GROUPS):
            carry = carry_s[g] + total_s[g]
            smallest = carry if smallest is None else jnp.minimum(smallest, carry)
        smallest = jnp.minimum(smallest[:, :K], smallest[:, K:])
        smallest = jnp.min(smallest.reshape(Q // V7X_SUBLANES, V7X_SUBLANES, K), axis=0)
        return jnp.min(smallest, axis=0, keepdims=True)[0, 0]

    def q_block(qi):
        q0 = pl.multiple_of(qi * Q, Q)
        qs = [q_ref[pl.ds(q0, Q), lanes[g]] for g in range(SB_GROUPS)]
        top = (qi + 1) * SB_STEP - 1
        group = lambda j: [top - j * SB_STEP - n for n in range(SB_STEP)]
        carry_s[...] = jnp.zeros_like(carry_s)
        acc_s[...] = jnp.zeros_like(acc_s)
        score(qs, q0, group(0), True)

        def trip(state):
            j, _ = state
            fold(group(j))
            score(qs, q0, group(j + 1), False)
            return j + 1, smallest_carry_past_scored()

        last = 0
        if not isinstance(qi, int):
            done, smallest = trip((0, None))
            last, _ = lax.while_loop(
                lambda state: (state[0] < qi) & (state[1] <= SB_DEAD_CARRY), trip,
                (jnp.int32(done), smallest))
        fold(group(last))

        for g in range(SB_GROUPS):
            o_ref[pl.ds(q0, Q), lanes[g]] = acc_s[g].astype(o_ref.dtype)

    q_block(0)
    lax.fori_loop(1, seq // Q, lambda qi, _: q_block(qi) or 0, 0)


def _sb_constants():
    r = lax.broadcasted_iota(jnp.int32, (2 * SB_KEYS, 2 * SB_KEYS), 0) % SB_KEYS
    c = lax.broadcasted_iota(jnp.int32, (2 * SB_KEYS, 2 * SB_KEYS), 1)
    return jnp.where((c >= SB_KEYS) | (r >= c), 1.0, 0.0).astype(BF16)


def _stick_breaking(q, k, v, *, batch, seq):
    assert seq % SB_QUERIES == 0 and SB_QUERIES == SB_STEP * SB_KEYS and 2 * SB_KEYS == V7X_MXU_DIM
    cs = _sb_constants()
    width = SB_GROUPS * V7X_LANES
    assert D_MODEL % width == 0
    group = pl.BlockSpec((None, seq, width), lambda b, g: (b, 0, g))
    per_head = pltpu.VMEM((SB_GROUPS, 2 * seq, V7X_LANES), BF16)
    scores = (SB_GROUPS, SB_QUERIES, 2 * SB_KEYS)
    resident = 2 * 4 * 2 * seq * width + 2 * cs.size + 2 * 2 * 2 * seq * width \
        + 4 * 8 * SB_GROUPS * SB_STEP * SB_QUERIES * 2 * SB_KEYS
    return pl.pallas_call(
        functools.partial(_sb_kernel, seq=seq),
        grid=(batch, D_MODEL // width),
        in_specs=[group, group, group, _resident(cs.shape)],
        out_specs=group,
        out_shape=jax.ShapeDtypeStruct((batch, seq, D_MODEL), BF16),
        scratch_shapes=[per_head, per_head,
                        pltpu.VMEM((SB_GROUPS, SB_STEP) + scores[1:], F32),
                        pltpu.VMEM(scores, F32), pltpu.VMEM(scores, F32),
                        pltpu.VMEM((SB_GROUPS, SB_QUERIES, V7X_LANES), F32)],
        compiler_params=pltpu.CompilerParams(
            dimension_semantics=("arbitrary", "arbitrary"),
            vmem_limit_bytes=_vmem_limit(resident)),
        name="stick_breaking",
    )(q, k, v, cs)


def _rglru_kernel(x_ref, g_ref, win_ref, cw_ref, cb_ref, wg_ref, br_ref, bi_ref, lam_ref, wo_ref,
                  o_ref, xt_s, ot_s, xb_s, y_s, ri_s, a_s, u_s, h_s, state_s, win_s, wo_s, *, batch):
    rows = LRU_STEPS * batch
    halo = CONV_W * batch
    lane_tiles = D_MODEL // V7X_LANES

    @pl.when(pl.program_id(0) == 0)
    def _():
        xb_s[:, 0:halo, :] = jnp.zeros((xb_s.shape[0], halo, V7X_MXU_DIM), F32)
        state_s[...] = jnp.zeros_like(state_s)
        win_s[...] = win_ref[...].astype(BF16)
        wo_s[...] = wo_ref[...].astype(BF16)

    for b in range(batch):
        for l in range(lane_tiles):
            xt_s[l, pl.ds(b, LRU_STEPS, stride=batch), :] = \
                x_ref[b, :, l * V7X_LANES:(l + 1) * V7X_LANES]
    x = jnp.concatenate([xt_s[l] for l in range(lane_tiles)], axis=1)
    xn = _rms_norm_rows(x, g_ref[...]).astype(BF16)
    lam = lam_ref[...]
    log_sig_lam = jnp.minimum(lam, 0.0) - jnp.log1p(jnp.exp(-jnp.abs(lam)))

    for g in range(D_MODEL // V7X_MXU_DIM):
        cols = slice(g * V7X_MXU_DIM, (g + 1) * V7X_MXU_DIM)
        xb_s[g, halo:halo + rows, :] = _dot(xn, win_s[:, cols])
        y_s[g] = _dot(xn, win_s[:, D_MODEL + cols.start:D_MODEL + cols.stop])
        xc = cb_ref[:, cols] + cw_ref[CONV_W - 1:CONV_W, cols] * xb_s[g, halo:halo + rows, :]
        for j in range(CONV_W - 1):
            off = halo - (CONV_W - 1 - j) * batch
            xc = xc + cw_ref[j:j + 1, cols] * xb_s[g, off:off + rows, :]
        u_s[g] = xc
        ri_s[g] = _dot(xc.astype(BF16), wg_ref[g])
        r = jax.nn.sigmoid(ri_s[g, :, :V7X_MXU_DIM] + br_ref[:, cols])
        i = jax.nn.sigmoid(ri_s[g, :, V7X_MXU_DIM:] + bi_ref[:, cols])
        log_a = (LRU_C * r) * log_sig_lam[:, cols]
        a_s[g] = jnp.exp(log_a)
        th = jnp.tanh(log_a)
        u_s[g] = jnp.sqrt(-2.0 * th / (1.0 - th)) * (i * u_s[g])
    xb_s[:, 0:halo, :] = xb_s[:, rows:rows + halo, :]

    def tiles(ref, sl):
        return jnp.concatenate([ref[g, sl, :] for g in range(ref.shape[0])], axis=1)

    def step(t, h):
        sl = pl.ds(pl.multiple_of(t * batch, batch), batch)
        h = tiles(a_s, sl) * h + tiles(u_s, sl)
        h_s[sl, :] = h
        return h

    state_s[...] = lax.fori_loop(0, LRU_STEPS, step, state_s[...], unroll=8)
    y = jax.nn.gelu(tiles(y_s, slice(None)), approximate=True)
    out = x + _dot((h_s[...] * y).astype(BF16), wo_s[...])
    for l in range(lane_tiles):
        ot_s[l] = out[:, l * V7X_LANES:(l + 1) * V7X_LANES]
    for b in range(batch):
        for l in range(lane_tiles):
            o_ref[b, :, l * V7X_LANES:(l + 1) * V7X_LANES] = \
                ot_s[l, pl.ds(b, LRU_STEPS, stride=batch), :]


def _block_diag_tiles(w):
    per = V7X_MXU_DIM // LRU_BLOCK_W
    w = w.reshape(LRU_BLOCKS // per, per, LRU_BLOCK_W, LRU_BLOCK_W)
    eye = jnp.eye(per, dtype=w.dtype)
    return jnp.einsum("gpcd,pq->gpcqd", w, eye).reshape(LRU_BLOCKS // per, V7X_MXU_DIM, V7X_MXU_DIM)


def _rglru(x, gain, w_in, conv_w, conv_b, w_r, b_r, w_i, b_i, lam, w_o):
    batch, seq, _ = x.shape
    assert batch == V7X_SUBLANES and seq % LRU_STEPS == 0
    rows = LRU_STEPS * batch
    w_gate = jnp.concatenate([_block_diag_tiles(w_r), _block_diag_tiles(w_i)], axis=-1).astype(BF16)
    vec = lambda p: p.reshape(1, D_MODEL)
    chunk = pl.BlockSpec((batch, LRU_STEPS, D_MODEL), lambda i: (0, i, 0))
    slabs = pltpu.VMEM((D_MODEL // V7X_LANES, rows, V7X_LANES), F32)
    tile_major = lambda r: pltpu.VMEM((D_MODEL // V7X_MXU_DIM, r, V7X_MXU_DIM), F32)
    resident = 6 * (w_in.size + w_o.size) + 2 * w_gate.size + 4 * 4 * rows * D_MODEL \
        + 4 * 9 * (rows + CONV_W * batch) * D_MODEL + 4 * 6 * rows * D_MODEL
    return pl.pallas_call(
        functools.partial(_rglru_kernel, batch=batch),
        grid=(seq // LRU_STEPS,),
        in_specs=[chunk, _resident((1, D_MODEL)), _resident(w_in.shape),
                  _resident((CONV_W, D_MODEL)), _resident((1, D_MODEL)),
                  _resident(w_gate.shape), _resident((1, D_MODEL)), _resident((1, D_MODEL)),
                  _resident((1, D_MODEL)), _resident(w_o.shape)],
        out_specs=chunk,
        out_shape=jax.ShapeDtypeStruct(x.shape, F32),
        scratch_shapes=[slabs, slabs,
                        tile_major(rows + CONV_W * batch),
                        tile_major(rows),
                        pltpu.VMEM((D_MODEL // V7X_MXU_DIM, rows, 2 * V7X_MXU_DIM), F32),
                        tile_major(rows), tile_major(rows),
                        pltpu.VMEM((rows, D_MODEL), F32),
                        pltpu.VMEM((batch, D_MODEL), F32),
                        pltpu.VMEM(w_in.shape, BF16), pltpu.VMEM(w_o.shape, BF16)],
        compiler_params=pltpu.CompilerParams(
            dimension_semantics=("arbitrary",), vmem_limit_bytes=_vmem_limit(resident)),
        name="rglru",
    )(x, vec(gain), w_in, conv_w, vec(conv_b), w_gate, vec(b_r), vec(b_i), vec(lam), w_o)


def kernel(x, l0_ff1_norm, l0_ff1_w_in, l0_ff1_w_out, l0_mix_norm, l0_sb_w_qkv, l0_sb_q_norm, l0_sb_k_norm, l0_sb_w_o, l0_ff2_norm, l0_ff2_w_in, l0_ff2_w_out, l1_ff1_norm, l1_ff1_w_in, l1_ff1_w_out, l1_mix_norm, l1_lru_w_in, l1_lru_conv_w, l1_lru_conv_b, l1_lru_w_r, l1_lru_b_r, l1_lru_w_i, l1_lru_b_i, l1_lru_lambda, l1_lru_w_o, l1_ff2_norm, l1_ff2_w_in, l1_ff2_w_out):
    batch, seq, d = x.shape
    assert d == D_MODEL
    n = batch * seq
    as_seq = lambda t: t.reshape(batch, seq, D_MODEL)
    x = x.reshape(n, D_MODEL)

    x = _ffn(x, l0_ff1_norm, l0_ff1_w_in, l0_ff1_w_out)
    q, k, v = _qkv(x, l0_mix_norm, l0_sb_w_qkv, l0_sb_q_norm, l0_sb_k_norm)
    o = _stick_breaking(as_seq(q), as_seq(k), as_seq(v), batch=batch, seq=seq)
    x = _ffn(x, l0_ff2_norm, l0_ff2_w_in, l0_ff2_w_out, mix=(o.reshape(n, D_MODEL), l0_sb_w_o))

    x = _ffn(x, l1_ff1_norm, l1_ff1_w_in, l1_ff1_w_out)
    x = _rglru(as_seq(x), l1_mix_norm, l1_lru_w_in, l1_lru_conv_w, l1_lru_conv_b,
               l1_lru_w_r, l1_lru_b_r, l1_lru_w_i, l1_lru_b_i, l1_lru_lambda, l1_lru_w_o)
    x = _ffn(x.reshape(n, D_MODEL), l1_ff2_norm, l1_ff2_w_in, l1_ff2_w_out)
    return as_seq(x)
```

```python
import functools
import math

import jax
import jax.numpy as jnp
from jax import lax
from jax.experimental import pallas as pl
from jax.experimental.pallas import tpu as pltpu

D_MODEL = 1024
SB_HEADS = 16
SB_HEAD_DIM = D_MODEL // SB_HEADS
LRU_BLOCKS = 16
LRU_BLOCK_W = D_MODEL // LRU_BLOCKS
LRU_C = 8.0
CONV_W = 4
NORM_EPS = 1e-6
LOG2_E = math.log2(math.e)

V7X_LANES = 128
V7X_SUBLANES = 8
V7X_MXU_DIM = 256
V7X_VMEM_BYTES = 64 * 1024 * 1024
VMEM_UNCLAIMED_BYTES = 4 * 1024 * 1024

F32 = jnp.float32
BF16 = jnp.bfloat16

FFN_ROWS = 512
FFN_CHUNK = V7X_MXU_DIM
PROJ_ROWS = 1024
SB_QUERIES = 256
SB_KEYS = 128
SB_STEP = SB_QUERIES // SB_KEYS
SB_GROUPS = 4
SB_MASKED_LOGIT = -1e30
SB_DEAD_CARRY = 128.0
LRU_STEPS = 64


def _vmem_limit(resident_bytes):
    return int(min(resident_bytes * 3 // 2, V7X_VMEM_BYTES - VMEM_UNCLAIMED_BYTES))


def _dot(a, b):
    return jnp.dot(a, b, preferred_element_type=F32)


def _rms_norm_rows(x, gain_row):
    ms = jnp.mean(x * x, axis=-1, keepdims=True)
    return x * lax.rsqrt(ms + NORM_EPS) * gain_row


def _split_bf16(v):
    hi = v.astype(BF16)
    lo = (v - hi.astype(F32)).astype(BF16)
    return hi, lo


def _resident(shape):
    zeros = (0,) * len(shape)
    return pl.BlockSpec(shape, lambda *_: zeros, pipeline_mode=pl.Buffered(1))


def _ffn_kernel(x_ref, g_ref, win_ref, wout_ref, *rest, d_ff):
    x = x_ref[...]
    if len(rest) == 3:
        a_ref, wmix_ref, o_ref = rest
        x = x + _dot(a_ref[...], wmix_ref[...].astype(BF16))
    else:
        o_ref, = rest
    xn = _rms_norm_rows(x, g_ref[...]).astype(BF16)
    y = jnp.zeros_like(x)
    for c in range(d_ff // FFN_CHUNK):
        lo = c * FFN_CHUNK
        gate = _dot(xn, win_ref[:, lo:lo + FFN_CHUNK].astype(BF16))
        up = _dot(xn, win_ref[:, d_ff + lo:d_ff + lo + FFN_CHUNK].astype(BF16))
        h = (gate * jax.nn.sigmoid(gate) * up).astype(BF16)
        y = y + _dot(h, wout_ref[lo:lo + FFN_CHUNK, :].astype(BF16))
    o_ref[...] = x + 0.5 * y


def _ffn(x, gain, w_in, w_out, mix=None):
    n = x.shape[0]
    d_ff = w_out.shape[0]
    assert d_ff % FFN_CHUNK == 0 and n % FFN_ROWS == 0
    row = pl.BlockSpec((FFN_ROWS, D_MODEL), lambda i: (i, 0))
    operands = [x, gain.reshape(1, D_MODEL), w_in, w_out]
    in_specs = [row, _resident((1, D_MODEL)), _resident(w_in.shape), _resident(w_out.shape)]
    resident = 4 * (w_in.size + w_out.size) + 4 * 4 * FFN_ROWS * D_MODEL \
        + 4 * FFN_ROWS * (3 * D_MODEL + 3 * FFN_CHUNK)
    if mix is not None:
        a, w_mix = mix
        operands += [a, w_mix]
        in_specs += [row, _resident(w_mix.shape)]
        resident += 4 * w_mix.size + 2 * 2 * FFN_ROWS * D_MODEL
    return pl.pallas_call(
        functools.partial(_ffn_kernel, d_ff=d_ff),
        grid=(n // FFN_ROWS,),
        in_specs=in_specs,
        out_specs=row,
        out_shape=jax.ShapeDtypeStruct((n, D_MODEL), F32),
        compiler_params=pltpu.CompilerParams(
            dimension_semantics=("arbitrary",), vmem_limit_bytes=_vmem_limit(resident)),
        name="ffn",
    )(*operands)


def _head_rms_norm(t, mean_ref, gain_row):
    outs = []
    for c in range(D_MODEL // V7X_MXU_DIM):
        tc = t[:, c * V7X_MXU_DIM:(c + 1) * V7X_MXU_DIM]
        hi, lo = _split_bf16(tc * tc)
        ms = _dot(hi, mean_ref[...]) + _dot(lo, mean_ref[...])
        outs.append(tc * lax.rsqrt(ms + NORM_EPS))
    return jnp.concatenate(outs, axis=-1) * gain_row


def _qkv_kernel(x_ref, g_ref, w_ref, mean_ref, qg_ref, kg_ref, q_ref, k_ref, v_ref):
    xn = _rms_norm_rows(x_ref[...], g_ref[...]).astype(BF16)
    q = _dot(xn, w_ref[:, 0:D_MODEL].astype(BF16))
    q_ref[...] = (_head_rms_norm(q, mean_ref, qg_ref[...]) * (1.0 / math.sqrt(SB_HEAD_DIM))).astype(BF16)
    k = _dot(xn, w_ref[:, D_MODEL:2 * D_MODEL].astype(BF16))
    k_ref[...] = _head_rms_norm(k, mean_ref, kg_ref[...]).astype(BF16)
    v_ref[...] = _dot(xn, w_ref[:, 2 * D_MODEL:3 * D_MODEL].astype(BF16)).astype(BF16)


def _head_mean_matrix():
    r = lax.broadcasted_iota(jnp.int32, (V7X_MXU_DIM, V7X_MXU_DIM), 0) // SB_HEAD_DIM
    c = lax.broadcasted_iota(jnp.int32, (V7X_MXU_DIM, V7X_MXU_DIM), 1) // SB_HEAD_DIM
    return jnp.where(r == c, 1.0 / SB_HEAD_DIM, 0.0).astype(BF16)


def _qkv(x, gain, w_qkv, q_gain, k_gain):
    n = x.shape[0]
    row = pl.BlockSpec((PROJ_ROWS, D_MODEL), lambda i: (i, 0))
    out = jax.ShapeDtypeStruct((n, D_MODEL), BF16)
    resident = 4 * w_qkv.size + 4 * 2 * PROJ_ROWS * D_MODEL + 3 * 2 * 2 * PROJ_ROWS * D_MODEL \
        + 4 * 4 * PROJ_ROWS * D_MODEL
    tile_gain = lambda g: jnp.tile(g, SB_HEADS).reshape(1, D_MODEL)
    return pl.pallas_call(
        _qkv_kernel,
        grid=(n // PROJ_ROWS,),
        in_specs=[row, _resident((1, D_MODEL)), _resident(w_qkv.shape),
                  _resident((V7X_MXU_DIM, V7X_MXU_DIM)),
                  _resident((1, D_MODEL)), _resident((1, D_MODEL))],
        out_specs=[row, row, row],
        out_shape=[out, out, out],
        compiler_params=pltpu.CompilerParams(
            dimension_semantics=("arbitrary",), vmem_limit_bytes=_vmem_limit(resident)),
        name="qkv",
    )(x, gain.reshape(1, D_MODEL), w_qkv, _head_mean_matrix(), tile_gain(q_gain), tile_gain(k_gain))


def _sb_kernel(q_ref, k_ref, v_ref, cs_ref, o_ref, kh_s, vh_s, x_s, total_s, carry_s, acc_s, *, seq):
    Q, K = SB_QUERIES, SB_KEYS
    lanes = [slice(g * V7X_LANES, (g + 1) * V7X_LANES) for g in range(SB_GROUPS)]

    lane = lax.broadcasted_iota(jnp.int32, (K, V7X_LANES), 1)
    head_lanes = [jnp.where((lane < SB_HEAD_DIM) == (h == 0), 1.0, 0.0).astype(BF16) for h in range(2)]

    def head_rows(kb, count=1):
        return pl.ds(pl.multiple_of(kb * 2 * K, 2 * K), count * 2 * K)

    def split_heads(kb, _):
        rows = pl.ds(pl.multiple_of(kb * K, K), K)
        for g in range(SB_GROUPS):
            for src, dst in ((k_ref, kh_s), (v_ref, vh_s)):
                blk = src[rows, lanes[g]]
                dst[g, head_rows(kb), :] = jnp.concatenate(
                    [blk * head_lanes[0], blk * head_lanes[1]], axis=0)
        return 0

    lax.fori_loop(0, seq // K, split_heads, 0)

    def score(qs, q0, kbs, diagonal):
        work = [(g, i, kb) for g in range(SB_GROUPS) for i, kb in enumerate(kbs)]
        first_row = lambda i: (len(kbs) - 1 - i) * K if diagonal else 0
        logits, s_parts = [], []
        for g, i, kb in work:
            r0 = first_row(i)
            z = lax.dot_general(qs[g][r0:], kh_s[g, head_rows(kb), :], (((1,), (1,)), ((), ())),
                                preferred_element_type=F32)
            s = jnp.maximum(z, 0.0) + jnp.log(1.0 + jnp.exp2(jnp.abs(z) * -LOG2_E))
            if diagonal:
                row = lax.broadcasted_iota(jnp.int32, (Q - r0, 2 * K), 0) + r0
                col = lax.broadcasted_iota(jnp.int32, (Q - r0, 2 * K), 1)
                earlier = (kb * K + (col & (K - 1))) < (q0 + row)
                s = jnp.where(earlier, s, 0.0)
                z = jnp.where(earlier, z, SB_MASKED_LOGIT)
            s_hi, s_lo = _split_bf16(s)
            for h in range(2):
                s_parts.append(jnp.concatenate([s_hi[:, h * K:(h + 1) * K],
                                                s_lo[:, h * K:(h + 1) * K]], axis=1))
            logits.append(z)
        sums = _dot(jnp.concatenate(s_parts, axis=0), cs_ref[...])
        offset = 0
        for n, (g, i, kb) in enumerate(work):
            r0 = first_row(i)
            h0 = sums[offset:offset + Q - r0]
            h1 = sums[offset + Q - r0:offset + 2 * (Q - r0)]
            offset += 2 * (Q - r0)
            from_key = jnp.concatenate([h0[:, :K], h1[:, :K]], axis=1)
            total = jnp.concatenate([h0[:, K:], h1[:, K:]], axis=1)
            x_s[g, i, r0:, :] = logits[n] - from_key if i == 0 else logits[n] - from_key - later[r0:]
            if r0:
                x_s[g, i, :r0, :] = jnp.full((r0, 2 * K), SB_MASKED_LOGIT, F32)
                total = jnp.concatenate([jnp.zeros((r0, 2 * K), F32), total], axis=0)
            later = total if i == 0 else later + total
            if i == len(kbs) - 1:
                total_s[g] = later

    def fold(kbs):
        for g in range(SB_GROUPS):
            carry = carry_s[g]
            w = [jnp.exp(x_s[g, i] - carry).astype(BF16) for i in reversed(range(len(kbs)))]
            values = vh_s[g, head_rows(kbs[-1], len(kbs)), :]
            acc_s[g] += _dot(jnp.concatenate(w, axis=1), values)
            carry_s[g] = carry + total_s[g]

    def smallest_carry_past_scored():
        smallest = None
        for g in range(SB_GROUPS):
            carry = carry_s[g] + total_s[g]
            smallest = carry if smallest is None else jnp.minimum(smallest, carry)
        smallest = jnp.minimum(smallest[:, :K], smallest[:, K:])
        smallest = jnp.min(smallest.reshape(Q // V7X_SUBLANES, V7X_SUBLANES, K), axis=0)
        return jnp.min(smallest, axis=0, keepdims=True)[0, 0]

    def q_block(qi):
        q0 = pl.multiple_of(qi * Q, Q)
        qs = [q_ref[pl.ds(q0, Q), lanes[g]] for g in range(SB_GROUPS)]
        top = (qi + 1) * SB_STEP - 1
        group = lambda j: [top - j * SB_STEP - n for n in range(SB_STEP)]
        carry_s[...] = jnp.zeros_like(carry_s)
        acc_s[...] = jnp.zeros_like(acc_s)
        score(qs, q0, group(0), True)

        def trip(state):
            j, _ = state
            fold(group(j))
            score(qs, q0, group(j + 1), False)
            return j + 1, smallest_carry_past_scored()

        last = 0
        if not isinstance(qi, int):
            done, smallest = trip((0, None))
            last, _ = lax.while_loop(
                lambda state: (state[0] < qi) & (state[1] <= SB_DEAD_CARRY), trip,
                (jnp.int32(done), smallest))
        fold(group(last))

        for g in range(SB_GROUPS):
            o_ref[pl.ds(q0, Q), lanes[g]] = acc_s[g].astype(o_ref.dtype)

    q_block(0)
    lax.fori_loop(1, seq // Q, lambda qi, _: q_block(qi) or 0, 0)


def _sb_constants():
    r = lax.broadcasted_iota(jnp.int32, (2 * SB_KEYS, 2 * SB_KEYS), 0) % SB_KEYS
    c = lax.broadcasted_iota(jnp.int32, (2 * SB_KEYS, 2 * SB_KEYS), 1)
    return jnp.where((c >= SB_KEYS) | (r >= c), 1.0, 0.0).astype(BF16)


def _stick_breaking(q, k, v, *, batch, seq):
    assert seq % SB_QUERIES == 0 and SB_QUERIES == SB_STEP * SB_KEYS and 2 * SB_KEYS == V7X_MXU_DIM
    cs = _sb_constants()
    width = SB_GROUPS * V7X_LANES
    assert D_MODEL % width == 0
    group = pl.BlockSpec((None, seq, width), lambda b, g: (b, 0, g))
    per_head = pltpu.VMEM((SB_GROUPS, 2 * seq, V7X_LANES), BF16)
    scores = (SB_GROUPS, SB_QUERIES, 2 * SB_KEYS)
    resident = 2 * 4 * 2 * seq * width + 2 * cs.size + 2 * 2 * 2 * seq * width \
        + 4 * 8 * SB_GROUPS * SB_STEP * SB_QUERIES * 2 * SB_KEYS
    return pl.pallas_call(
        functools.partial(_sb_kernel, seq=seq),
        grid=(batch, D_MODEL // width),
        in_specs=[group, group, group, _resident(cs.shape)],
        out_specs=group,
        out_shape=jax.ShapeDtypeStruct((batch, seq, D_MODEL), BF16),
        scratch_shapes=[per_head, per_head,
                        pltpu.VMEM((SB_GROUPS, SB_STEP) + scores[1:], F32),
                        pltpu.VMEM(scores, F32), pltpu.VMEM(scores, F32),
                        pltpu.VMEM((SB_GROUPS, SB_QUERIES, V7X_LANES), F32)],
        compiler_params=pltpu.CompilerParams(
            dimension_semantics=("arbitrary", "arbitrary"),
            vmem_limit_bytes=_vmem_limit(resident)),
        name="stick_breaking",
    )(q, k, v, cs)


def _rglru_kernel(x_ref, g_ref, win_ref, cw_ref, cb_ref, wg_ref, br_ref, bi_ref, lam_ref, wo_ref,
                  o_ref, xt_s, ot_s, xb_s, y_s, ri_s, a_s, u_s, h_s, state_s, win_s, wo_s, *, batch):
    rows = LRU_STEPS * batch
    halo = CONV_W * batch
    lane_tiles = D_MODEL // V7X_LANES

    @pl.when(pl.program_id(0) == 0)
    def _():
        xb_s[:, 0:halo, :] = jnp.zeros((xb_s.shape[0], halo, V7X_MXU_DIM), F32)
        state_s[...] = jnp.zeros_like(state_s)
        win_s[...] = win_ref[...].astype(BF16)
        wo_s[...] = wo_ref[...].astype(BF16)

    for b in range(batch):
        for l in range(lane_tiles):
            xt_s[l, pl.ds(b, LRU_STEPS, stride=batch), :] = \
                x_ref[b, :, l * V7X_LANES:(l + 1) * V7X_LANES]
    x = jnp.concatenate([xt_s[l] for l in range(lane_tiles)], axis=1)
    xn = _rms_norm_rows(x, g_ref[...]).astype(BF16)
    lam = lam_ref[...]
    log_sig_lam = jnp.minimum(lam, 0.0) - jnp.log1p(jnp.exp(-jnp.abs(lam)))

    for g in range(D_MODEL // V7X_MXU_DIM):
        cols = slice(g * V7X_MXU_DIM, (g + 1) * V7X_MXU_DIM)
        xb_s[g, halo:halo + rows, :] = _dot(xn, win_s[:, cols])
        y_s[g] = _dot(xn, win_s[:, D_MODEL + cols.start:D_MODEL + cols.stop])
        xc = cb_ref[:, cols] + cw_ref[CONV_W - 1:CONV_W, cols] * xb_s[g, halo:halo + rows, :]
        for j in range(CONV_W - 1):
            off = halo - (CONV_W - 1 - j) * batch
            xc = xc + cw_ref[j:j + 1, cols] * xb_s[g, off:off + rows, :]
        u_s[g] = xc
        ri_s[g] = _dot(xc.astype(BF16), wg_ref[g])
        r = jax.nn.sigmoid(ri_s[g, :, :V7X_MXU_DIM] + br_ref[:, cols])
        i = jax.nn.sigmoid(ri_s[g, :, V7X_MXU_DIM:] + bi_ref[:, cols])
        log_a = (LRU_C * r) * log_sig_lam[:, cols]
        a_s[g] = jnp.exp(log_a)
        th = jnp.tanh(log_a)
        u_s[g] = jnp.sqrt(-2.0 * th / (1.0 - th)) * (i * u_s[g])
    xb_s[:, 0:halo, :] = xb_s[:, rows:rows + halo, :]

    def tiles(ref, sl):
        return jnp.concatenate([ref[g, sl, :] for g in range(ref.shape[0])], axis=1)

    def step(t, h):
        sl = pl.ds(pl.multiple_of(t * batch, batch), batch)
        h = tiles(a_s, sl) * h + tiles(u_s, sl)
        h_s[sl, :] = h
        return h

    state_s[...] = lax.fori_loop(0, LRU_STEPS, step, state_s[...], unroll=8)
    y = jax.nn.gelu(tiles(y_s, slice(None)), approximate=True)
    out = x + _dot((h_s[...] * y).astype(BF16), wo_s[...])
    for l in range(lane_tiles):
        ot_s[l] = out[:, l * V7X_LANES:(l + 1) * V7X_LANES]
    for b in range(batch):
        for l in range(lane_tiles):
            o_ref[b, :, l * V7X_LANES:(l + 1) * V7X_LANES] = \
                ot_s[l, pl.ds(b, LRU_STEPS, stride=batch), :]


def _block_diag_tiles(w):
    per = V7X_MXU_DIM // LRU_BLOCK_W
    w = w.reshape(LRU_BLOCKS // per, per, LRU_BLOCK_W, LRU_BLOCK_W)
    eye = jnp.eye(per, dtype=w.dtype)
    return jnp.einsum("gpcd,pq->gpcqd", w, eye).reshape(LRU_BLOCKS // per, V7X_MXU_DIM, V7X_MXU_DIM)


def _rglru(x, gain, w_in, conv_w, conv_b, w_r, b_r, w_i, b_i, lam, w_o):
    batch, seq, _ = x.shape
    assert batch == V7X_SUBLANES and seq % LRU_STEPS == 0
    rows = LRU_STEPS * batch
    w_gate = jnp.concatenate([_block_diag_tiles(w_r), _block_diag_tiles(w_i)], axis=-1).astype(BF16)
    vec = lambda p: p.reshape(1, D_MODEL)
    chunk = pl.BlockSpec((batch, LRU_STEPS, D_MODEL), lambda i: (0, i, 0))
    slabs = pltpu.VMEM((D_MODEL // V7X_LANES, rows, V7X_LANES), F32)
    tile_major = lambda r: pltpu.VMEM((D_MODEL // V7X_MXU_DIM, r, V7X_MXU_DIM), F32)
    resident = 6 * (w_in.size + w_o.size) + 2 * w_gate.size + 4 * 4 * rows * D_MODEL \
        + 4 * 9 * (rows + CONV_W * batch) * D_MODEL + 4 * 6 * rows * D_MODEL
    return pl.pallas_call(
        functools.partial(_rglru_kernel, batch=batch),
        grid=(seq // LRU_STEPS,),
        in_specs=[chunk, _resident((1, D_MODEL)), _resident(w_in.shape),
                  _resident((CONV_W, D_MODEL)), _resident((1, D_MODEL)),
                  _resident(w_gate.shape), _resident((1, D_MODEL)), _resident((1, D_MODEL)),
                  _resident((1, D_MODEL)), _resident(w_o.shape)],
        out_specs=chunk,
        out_shape=jax.ShapeDtypeStruct(x.shape, F32),
        scratch_shapes=[slabs, slabs,
                        tile_major(rows + CONV_W * batch),
                        tile_major(rows),
                        pltpu.VMEM((D_MODEL // V7X_MXU_DIM, rows, 2 * V7X_MXU_DIM), F32),
                        tile_major(rows), tile_major(rows),
                        pltpu.VMEM((rows, D_MODEL), F32),
                        pltpu.VMEM((batch, D_MODEL), F32),
                        pltpu.VMEM(w_in.shape, BF16), pltpu.VMEM(w_o.shape, BF16)],
        compiler_params=pltpu.CompilerParams(
            dimension_semantics=("arbitrary",), vmem_limit_bytes=_vmem_limit(resident)),
        name="rglru",
    )(x, vec(gain), w_in, conv_w, vec(conv_b), w_gate, vec(b_r), vec(b_i), vec(lam), w_o)


def kernel(x, l0_ff1_norm, l0_ff1_w_in, l0_ff1_w_out, l0_mix_norm, l0_sb_w_qkv, l0_sb_q_norm, l0_sb_k_norm, l0_sb_w_o, l0_ff2_norm, l0_ff2_w_in, l0_ff2_w_out, l1_ff1_norm, l1_ff1_w_in, l1_ff1_w_out, l1_mix_norm, l1_lru_w_in, l1_lru_conv_w, l1_lru_conv_b, l1_lru_w_r, l1_lru_b_r, l1_lru_w_i, l1_lru_b_i, l1_lru_lambda, l1_lru_w_o, l1_ff2_norm, l1_ff2_w_in, l1_ff2_w_out):
    batch, seq, d = x.shape
    assert d == D_MODEL
    n = batch * seq
    as_seq = lambda t: t.reshape(batch, seq, D_MODEL)
    x = x.reshape(n, D_MODEL)

    x = _ffn(x, l0_ff1_norm, l0_ff1_w_in, l0_ff1_w_out)
    q, k, v = _qkv(x, l0_mix_norm, l0_sb_w_qkv, l0_sb_q_norm, l0_sb_k_norm)
    o = _stick_breaking(as_seq(q), as_seq(k), as_seq(v), batch=batch, seq=seq)
    x = _ffn(x, l0_ff2_norm, l0_ff2_w_in, l0_ff2_w_out, mix=(o.reshape(n, D_MODEL), l0_sb_w_o))

    x = _ffn(x, l1_ff1_norm, l1_ff1_w_in, l1_ff1_w_out)
    x = _rglru(as_seq(x), l1_mix_norm, l1_lru_w_in, l1_lru_conv_w, l1_lru_conv_b,
               l1_lru_w_r, l1_lru_b_r, l1_lru_w_i, l1_lru_b_i, l1_lru_lambda, l1_lru_w_o)
    x = _ffn(x.reshape(n, D_MODEL), l1_ff2_norm, l1_ff2_w_in, l1_ff2_w_out)
    return as_seq(x)
```

```python
import functools
import math

import jax
import jax.numpy as jnp
from jax import lax
from jax.experimental import pallas as pl
from jax.experimental.pallas import tpu as pltpu

D_MODEL = 1024
SB_HEADS = 16
SB_HEAD_DIM = D_MODEL // SB_HEADS
LRU_BLOCKS = 16
LRU_BLOCK_W = D_MODEL // LRU_BLOCKS
LRU_C = 8.0
CONV_W = 4
NORM_EPS = 1e-6
LOG2_E = math.log2(math.e)

V7X_LANES = 128
V7X_SUBLANES = 8
V7X_MXU_DIM = 256
V7X_VMEM_BYTES = 64 * 1024 * 1024
VMEM_UNCLAIMED_BYTES = 4 * 1024 * 1024

F32 = jnp.float32
BF16 = jnp.bfloat16

FFN_ROWS = 512
FFN_CHUNK = V7X_MXU_DIM
PROJ_ROWS = 1024
SB_QUERIES = 256
SB_KEYS = 128
SB_STEP = SB_QUERIES // SB_KEYS
SB_GROUPS = 4
SB_MASKED_LOGIT = -1e30
SB_DEAD_CARRY = 128.0
LRU_STEPS = 64


def _vmem_limit(resident_bytes):
    return int(min(resident_bytes * 3 // 2, V7X_VMEM_BYTES - VMEM_UNCLAIMED_BYTES))


def _dot(a, b):
    return jnp.dot(a, b, preferred_element_type=F32)


def _rms_norm_rows(x, gain_row):
    ms = jnp.mean(x * x, axis=-1, keepdims=True)
    return x * lax.rsqrt(ms + NORM_EPS) * gain_row


def _split_bf16(v):
    hi = v.astype(BF16)
    lo = (v - hi.astype(F32)).astype(BF16)
    return hi, lo


def _resident(shape):
    zeros = (0,) * len(shape)
    return pl.BlockSpec(shape, lambda *_: zeros, pipeline_mode=pl.Buffered(1))


def _ffn_kernel(x_ref, g_ref, win_ref, wout_ref, *rest, d_ff):
    x = x_ref[...]
    if len(rest) == 3:
        a_ref, wmix_ref, o_ref = rest
        x = x + _dot(a_ref[...], wmix_ref[...].astype(BF16))
    else:
        o_ref, = rest
    xn = _rms_norm_rows(x, g_ref[...]).astype(BF16)
    y = jnp.zeros_like(x)
    for c in range(d_ff // FFN_CHUNK):
        lo = c * FFN_CHUNK
        gate = _dot(xn, win_ref[:, lo:lo + FFN_CHUNK].astype(BF16))
        up = _dot(xn, win_ref[:, d_ff + lo:d_ff + lo + FFN_CHUNK].astype(BF16))
        h = (gate * jax.nn.sigmoid(gate) * up).astype(BF16)
        y = y + _dot(h, wout_ref[lo:lo + FFN_CHUNK, :].astype(BF16))
    o_ref[...] = x + 0.5 * y


def _ffn(x, gain, w_in, w_out, mix=None):
    n = x.shape[0]
    d_ff = w_out.shape[0]
    assert d_ff % FFN_CHUNK == 0 and n % FFN_ROWS == 0
    row = pl.BlockSpec((FFN_ROWS, D_MODEL), lambda i: (i, 0))
    operands = [x, gain.reshape(1, D_MODEL), w_in, w_out]
    in_specs = [row, _resident((1, D_MODEL)), _resident(w_in.shape), _resident(w_out.shape)]
    resident = 4 * (w_in.size + w_out.size) + 4 * 4 * FFN_ROWS * D_MODEL \
        + 4 * FFN_ROWS * (3 * D_MODEL + 3 * FFN_CHUNK)
    if mix is not None:
        a, w_mix = mix
        operands += [a, w_mix]
        in_specs += [row, _resident(w_mix.shape)]
        resident += 4 * w_mix.size + 2 * 2 * FFN_ROWS * D_MODEL
    return pl.pallas_call(
        functools.partial(_ffn_kernel, d_ff=d_ff),
        grid=(n // FFN_ROWS,),
        in_specs=in_specs,
        out_specs=row,
        out_shape=jax.ShapeDtypeStruct((n, D_MODEL), F32),
        compiler_params=pltpu.CompilerParams(
            dimension_semantics=("arbitrary",), vmem_limit_bytes=_vmem_limit(resident)),
        name="ffn",
    )(*operands)


def _head_rms_norm(t, mean_ref, gain_row):
    outs = []
    for c in range(D_MODEL // V7X_MXU_DIM):
        tc = t[:, c * V7X_MXU_DIM:(c + 1) * V7X_MXU_DIM]
        hi, lo = _split_bf16(tc * tc)
        ms = _dot(hi, mean_ref[...]) + _dot(lo, mean_ref[...])
        outs.append(tc * lax.rsqrt(ms + NORM_EPS))
    return jnp.concatenate(outs, axis=-1) * gain_row


def _qkv_kernel(x_ref, g_ref, w_ref, mean_ref, qg_ref, kg_ref, q_ref, k_ref, v_ref):
    xn = _rms_norm_rows(x_ref[...], g_ref[...]).astype(BF16)
    q = _dot(xn, w_ref[:, 0:D_MODEL].astype(BF16))
    q_ref[...] = (_head_rms_norm(q, mean_ref, qg_ref[...]) * (1.0 / math.sqrt(SB_HEAD_DIM))).astype(BF16)
    k = _dot(xn, w_ref[:, D_MODEL:2 * D_MODEL].astype(BF16))
    k_ref[...] = _head_rms_norm(k, mean_ref, kg_ref[...]).astype(BF16)
    v_ref[...] = _dot(xn, w_ref[:, 2 * D_MODEL:3 * D_MODEL].astype(BF16)).astype(BF16)


def _head_mean_matrix():
    r = lax.broadcasted_iota(jnp.int32, (V7X_MXU_DIM, V7X_MXU_DIM), 0) // SB_HEAD_DIM
    c = lax.broadcasted_iota(jnp.int32, (V7X_MXU_DIM, V7X_MXU_DIM), 1) // SB_HEAD_DIM
    return jnp.where(r == c, 1.0 / SB_HEAD_DIM, 0.0).astype(BF16)


def _qkv(x, gain, w_qkv, q_gain, k_gain):
    n = x.shape[0]
    row = pl.BlockSpec((PROJ_ROWS, D_MODEL), lambda i: (i, 0))
    out = jax.ShapeDtypeStruct((n, D_MODEL), BF16)
    resident = 4 * w_qkv.size + 4 * 2 * PROJ_ROWS * D_MODEL + 3 * 2 * 2 * PROJ_ROWS * D_MODEL \
        + 4 * 4 * PROJ_ROWS * D_MODEL
    tile_gain = lambda g: jnp.tile(g, SB_HEADS).reshape(1, D_MODEL)
    return pl.pallas_call(
        _qkv_kernel,
        grid=(n // PROJ_ROWS,),
        in_specs=[row, _resident((1, D_MODEL)), _resident(w_qkv.shape),
                  _resident((V7X_MXU_DIM, V7X_MXU_DIM)),
                  _resident((1, D_MODEL)), _resident((1, D_MODEL))],
        out_specs=[row, row, row],
        out_shape=[out, out, out],
        compiler_params=pltpu.CompilerParams(
            dimension_semantics=("arbitrary",), vmem_limit_bytes=_vmem_limit(resident)),
        name="qkv",
    )(x, gain.reshape(1, D_MODEL), w_qkv, _head_mean_matrix(), tile_gain(q_gain), tile_gain(k_gain))


def _sb_kernel(q_ref, k_ref, v_ref, cs_ref, o_ref, kh_s, vh_s, x_s, total_s, carry_s, acc_s, *, seq):
    Q, K = SB_QUERIES, SB_KEYS
    lanes = [slice(g * V7X_LANES, (g + 1) * V7X_LANES) for g in range(SB_GROUPS)]

    lane = lax.broadcasted_iota(jnp.int32, (K, V7X_LANES), 1)
    head_lanes = [jnp.where((lane < SB_HEAD_DIM) == (h == 0), 1.0, 0.0).astype(BF16) for h in range(2)]

    def head_rows(kb, count=1):
        return pl.ds(pl.multiple_of(kb * 2 * K, 2 * K), count * 2 * K)

    def split_heads(kb, _):
        rows = pl.ds(pl.multiple_of(kb * K, K), K)
        for g in range(SB_GROUPS):
            for src, dst in ((k_ref, kh_s), (v_ref, vh_s)):
                blk = src[rows, lanes[g]]
                dst[g, head_rows(kb), :] = jnp.concatenate(
                    [blk * head_lanes[0], blk * head_lanes[1]], axis=0)
        return 0

    lax.fori_loop(0, seq // K, split_heads, 0)

    def score(qs, kbs, diagonal):
        work = [(g, i, kb) for g in range(SB_GROUPS) for i, kb in enumerate(kbs)]
        first_row = lambda i: (len(kbs) - 1 - i) * K if diagonal else 0
        logits, s_parts = [], []
        for g, i, kb in work:
            r0 = first_row(i)
            z = lax.dot_general(qs[g][r0:], kh_s[g, head_rows(kb), :], (((1,), (1,)), ((), ())),
                                preferred_element_type=F32)
            s = jnp.maximum(z, 0.0) + jnp.log(1.0 + jnp.exp2(jnp.abs(z) * -LOG2_E))
            if diagonal:
                row = lax.broadcasted_iota(jnp.int32, (Q - r0, 2 * K), 0)
                col = lax.broadcasted_iota(jnp.int32, (Q - r0, 2 * K), 1)
                earlier = (col & (K - 1)) < row
                s = jnp.where(earlier, s, 0.0)
                z = jnp.where(earlier, z, SB_MASKED_LOGIT)
            s_hi, s_lo = _split_bf16(s)
            for h in range(2):
                s_parts.append(jnp.concatenate([s_hi[:, h * K:(h + 1) * K],
                                                s_lo[:, h * K:(h + 1) * K]], axis=1))
            logits.append(z)
        sums = _dot(jnp.concatenate(s_parts, axis=0), cs_ref[...])
        offset = 0
        for n, (g, i, kb) in enumerate(work):
            r0 = first_row(i)
            h0 = sums[offset:offset + Q - r0]
            h1 = sums[offset + Q - r0:offset + 2 * (Q - r0)]
            offset += 2 * (Q - r0)
            from_key = jnp.concatenate([h0[:, :K], h1[:, :K]], axis=1)
            total = jnp.concatenate([h0[:, K:], h1[:, K:]], axis=1)
            x_s[g, i, r0:, :] = logits[n] - from_key if i == 0 else logits[n] - from_key - later[r0:]
            if r0:
                x_s[g, i, :r0, :] = jnp.full((r0, 2 * K), SB_MASKED_LOGIT, F32)
                total = jnp.concatenate([jnp.zeros((r0, 2 * K), F32), total], axis=0)
            later = total if i == 0 else later + total
            if i == len(kbs) - 1:
                total_s[g] = later

    def fold(kbs):
        for g in range(SB_GROUPS):
            carry = carry_s[g]
            w = [jnp.exp(x_s[g, i] - carry).astype(BF16) for i in reversed(range(len(kbs)))]
            values = vh_s[g, head_rows(kbs[-1], len(kbs)), :]
            acc_s[g] += _dot(jnp.concatenate(w, axis=1), values)
            carry_s[g] = carry + total_s[g]

    def smallest_carry():
        smallest = None
        for g in range(SB_GROUPS):
            carry = carry_s[g]
            smallest = carry if smallest is None else jnp.minimum(smallest, carry)
        smallest = jnp.minimum(smallest[:, :K], smallest[:, K:])
        smallest = jnp.min(smallest.reshape(Q // V7X_SUBLANES, V7X_SUBLANES, K), axis=0)
        return jnp.min(smallest, axis=0, keepdims=True)[0, 0]

    def q_block(qi):
        q0 = pl.multiple_of(qi * Q, Q)
        qs = [q_ref[pl.ds(q0, Q), lanes[g]] for g in range(SB_GROUPS)]
        top = (qi + 1) * SB_STEP - 1
        group = lambda j: [top - j * SB_STEP - n for n in range(SB_STEP)]
        carry_s[...] = jnp.zeros_like(carry_s)
        acc_s[...] = jnp.zeros_like(acc_s)
        score(qs, group(0), True)
        fold(group(0))

        def sweep(j):
            score(qs, group(j), False)
            fold(group(j))
            return j, smallest_carry()

        if not isinstance(qi, int):
            lax.while_loop(lambda state: (state[0] < qi) & (state[1] <= SB_DEAD_CARRY),
                           lambda state: sweep(state[0] + 1), (jnp.int32(1), sweep(1)[1]))

        for g in range(SB_GROUPS):
            o_ref[pl.ds(q0, Q), lanes[g]] = acc_s[g].astype(o_ref.dtype)

    q_block(0)
    lax.fori_loop(1, seq // Q, lambda qi, _: q_block(qi) or 0, 0)


def _sb_constants():
    r = lax.broadcasted_iota(jnp.int32, (2 * SB_KEYS, 2 * SB_KEYS), 0) % SB_KEYS
    c = lax.broadcasted_iota(jnp.int32, (2 * SB_KEYS, 2 * SB_KEYS), 1)
    return jnp.where((c >= SB_KEYS) | (r >= c), 1.0, 0.0).astype(BF16)


def _stick_breaking(q, k, v, *, batch, seq):
    assert seq % SB_QUERIES == 0 and SB_QUERIES == SB_STEP * SB_KEYS and 2 * SB_KEYS == V7X_MXU_DIM
    cs = _sb_constants()
    width = SB_GROUPS * V7X_LANES
    assert D_MODEL % width == 0
    group = pl.BlockSpec((None, seq, width), lambda b, g: (b, 0, g))
    per_head = pltpu.VMEM((SB_GROUPS, 2 * seq, V7X_LANES), BF16)
    scores = (SB_GROUPS, SB_QUERIES, 2 * SB_KEYS)
    resident = 2 * 4 * 2 * seq * width + 2 * cs.size + 2 * 2 * 2 * seq * width \
        + 4 * 8 * SB_GROUPS * SB_STEP * SB_QUERIES * 2 * SB_KEYS
    return pl.pallas_call(
        functools.partial(_sb_kernel, seq=seq),
        grid=(batch, D_MODEL // width),
        in_specs=[group, group, group, _resident(cs.shape)],
        out_specs=group,
        out_shape=jax.ShapeDtypeStruct((batch, seq, D_MODEL), BF16),
        scratch_shapes=[per_head, per_head,
                        pltpu.VMEM((SB_GROUPS, SB_STEP) + scores[1:], F32),
                        pltpu.VMEM(scores, F32), pltpu.VMEM(scores, F32),
                        pltpu.VMEM((SB_GROUPS, SB_QUERIES, V7X_LANES), F32)],
        compiler_params=pltpu.CompilerParams(
            dimension_semantics=("arbitrary", "arbitrary"),
            vmem_limit_bytes=_vmem_limit(resident)),
        name="stick_breaking",
    )(q, k, v, cs)


def _rglru_kernel(x_ref, g_ref, win_ref, cw_ref, cb_ref, wg_ref, br_ref, bi_ref, lam_ref, wo_ref,
                  o_ref, xt_s, ot_s, xb_s, y_s, ri_s, a_s, u_s, h_s, state_s, win_s, wo_s, *, batch):
    rows = LRU_STEPS * batch
    halo = CONV_W * batch
    lane_tiles = D_MODEL // V7X_LANES

    @pl.when(pl.program_id(0) == 0)
    def _():
        xb_s[:, 0:halo, :] = jnp.zeros((xb_s.shape[0], halo, V7X_MXU_DIM), F32)
        state_s[...] = jnp.zeros_like(state_s)
        win_s[...] = win_ref[...].astype(BF16)
        wo_s[...] = wo_ref[...].astype(BF16)

    for b in range(batch):
        for l in range(lane_tiles):
            xt_s[l, pl.ds(b, LRU_STEPS, stride=batch), :] = \
                x_ref[b, :, l * V7X_LANES:(l + 1) * V7X_LANES]
    x = jnp.concatenate([xt_s[l] for l in range(lane_tiles)], axis=1)
    xn = _rms_norm_rows(x, g_ref[...]).astype(BF16)
    lam = lam_ref[...]
    log_sig_lam = jnp.minimum(lam, 0.0) - jnp.log1p(jnp.exp(-jnp.abs(lam)))

    for g in range(D_MODEL // V7X_MXU_DIM):
        cols = slice(g * V7X_MXU_DIM, (g + 1) * V7X_MXU_DIM)
        xb_s[g, halo:halo + rows, :] = _dot(xn, win_s[:, cols])
        y_s[g] = _dot(xn, win_s[:, D_MODEL + cols.start:D_MODEL + cols.stop])
        xc = cb_ref[:, cols] + cw_ref[CONV_W - 1:CONV_W, cols] * xb_s[g, halo:halo + rows, :]
        for j in range(CONV_W - 1):
            off = halo - (CONV_W - 1 - j) * batch
            xc = xc + cw_ref[j:j + 1, cols] * xb_s[g, off:off + rows, :]
        u_s[g] = xc
        ri_s[g] = _dot(xc.astype(BF16), wg_ref[g])
        r = jax.nn.sigmoid(ri_s[g, :, :V7X_MXU_DIM] + br_ref[:, cols])
        i = jax.nn.sigmoid(ri_s[g, :, V7X_MXU_DIM:] + bi_ref[:, cols])
        log_a = (LRU_C * r) * log_sig_lam[:, cols]
        a_s[g] = jnp.exp(log_a)
        th = jnp.tanh(log_a)
        u_s[g] = jnp.sqrt(-2.0 * th / (1.0 - th)) * (i * u_s[g])
    xb_s[:, 0:halo, :] = xb_s[:, rows:rows + halo, :]

    def tiles(ref, sl):
        return jnp.concatenate([ref[g, sl, :] for g in range(ref.shape[0])], axis=1)

    def step(t, h):
        sl = pl.ds(pl.multiple_of(t * batch, batch), batch)
        h = tiles(a_s, sl) * h + tiles(u_s, sl)
        h_s[sl, :] = h
        return h

    state_s[...] = lax.fori_loop(0, LRU_STEPS, step, state_s[...], unroll=8)
    y = jax.nn.gelu(tiles(y_s, slice(None)), approximate=True)
    out = x + _dot((h_s[...] * y).astype(BF16), wo_s[...])
    for l in range(lane_tiles):
        ot_s[l] = out[:, l * V7X_LANES:(l + 1) * V7X_LANES]
    for b in range(batch):
        for l in range(lane_tiles):
            o_ref[b, :, l * V7X_LANES:(l + 1) * V7X_LANES] = \
                ot_s[l, pl.ds(b, LRU_STEPS, stride=batch), :]


def _block_diag_tiles(w):
    per = V7X_MXU_DIM // LRU_BLOCK_W
    w = w.reshape(LRU_BLOCKS // per, per, LRU_BLOCK_W, LRU_BLOCK_W)
    eye = jnp.eye(per, dtype=w.dtype)
    return jnp.einsum("gpcd,pq->gpcqd", w, eye).reshape(LRU_BLOCKS // per, V7X_MXU_DIM, V7X_MXU_DIM)


def _rglru(x, gain, w_in, conv_w, conv_b, w_r, b_r, w_i, b_i, lam, w_o):
    batch, seq, _ = x.shape
    assert batch == V7X_SUBLANES and seq % LRU_STEPS == 0
    rows = LRU_STEPS * batch
    w_gate = jnp.concatenate([_block_diag_tiles(w_r), _block_diag_tiles(w_i)], axis=-1).astype(BF16)
    vec = lambda p: p.reshape(1, D_MODEL)
    chunk = pl.BlockSpec((batch, LRU_STEPS, D_MODEL), lambda i: (0, i, 0))
    slabs = pltpu.VMEM((D_MODEL // V7X_LANES, rows, V7X_LANES), F32)
    tile_major = lambda r: pltpu.VMEM((D_MODEL // V7X_MXU_DIM, r, V7X_MXU_DIM), F32)
    resident = 6 * (w_in.size + w_o.size) + 2 * w_gate.size + 4 * 4 * rows * D_MODEL \
        + 4 * 9 * (rows + CONV_W * batch) * D_MODEL + 4 * 6 * rows * D_MODEL
    return pl.pallas_call(
        functools.partial(_rglru_kernel, batch=batch),
        grid=(seq // LRU_STEPS,),
        in_specs=[chunk, _resident((1, D_MODEL)), _resident(w_in.shape),
                  _resident((CONV_W, D_MODEL)), _resident((1, D_MODEL)),
                  _resident(w_gate.shape), _resident((1, D_MODEL)), _resident((1, D_MODEL)),
                  _resident((1, D_MODEL)), _resident(w_o.shape)],
        out_specs=chunk,
        out_shape=jax.ShapeDtypeStruct(x.shape, F32),
        scratch_shapes=[slabs, slabs,
                        tile_major(rows + CONV_W * batch),
                        tile_major(rows),
                        pltpu.VMEM((D_MODEL // V7X_MXU_DIM, rows, 2 * V7X_MXU_DIM), F32),
                        tile_major(rows), tile_major(rows),
                        pltpu.VMEM((rows, D_MODEL), F32),
                        pltpu.VMEM((batch, D_MODEL), F32),
                        pltpu.VMEM(w_in.shape, BF16), pltpu.VMEM(w_o.shape, BF16)],
        compiler_params=pltpu.CompilerParams(
            dimension_semantics=("arbitrary",), vmem_limit_bytes=_vmem_limit(resident)),
        name="rglru",
    )(x, vec(gain), w_in, conv_w, vec(conv_b), w_gate, vec(b_r), vec(b_i), vec(lam), w_o)


def kernel(x, l0_ff1_norm, l0_ff1_w_in, l0_ff1_w_out, l0_mix_norm, l0_sb_w_qkv, l0_sb_q_norm, l0_sb_k_norm, l0_sb_w_o, l0_ff2_norm, l0_ff2_w_in, l0_ff2_w_out, l1_ff1_norm, l1_ff1_w_in, l1_ff1_w_out, l1_mix_norm, l1_lru_w_in, l1_lru_conv_w, l1_lru_conv_b, l1_lru_w_r, l1_lru_b_r, l1_lru_w_i, l1_lru_b_i, l1_lru_lambda, l1_lru_w_o, l1_ff2_norm, l1_ff2_w_in, l1_ff2_w_out):
    batch, seq, d = x.shape
    assert d == D_MODEL
    n = batch * seq
    as_seq = lambda t: t.reshape(batch, seq, D_MODEL)
    x = x.reshape(n, D_MODEL)

    x = _ffn(x, l0_ff1_norm, l0_ff1_w_in, l0_ff1_w_out)
    q, k, v = _qkv(x, l0_mix_norm, l0_sb_w_qkv, l0_sb_q_norm, l0_sb_k_norm)
    o = _stick_breaking(as_seq(q), as_seq(k), as_seq(v), batch=batch, seq=seq)
    x = _ffn(x, l0_ff2_norm, l0_ff2_w_in, l0_ff2_w_out, mix=(o.reshape(n, D_MODEL), l0_sb_w_o))

    x = _ffn(x, l1_ff1_norm, l1_ff1_w_in, l1_ff1_w_out)
    x = _rglru(as_seq(x), l1_mix_norm, l1_lru_w_in, l1_lru_conv_w, l1_lru_conv_b,
               l1_lru_w_r, l1_lru_b_r, l1_lru_w_i, l1_lru_b_i, l1_lru_lambda, l1_lru_w_o)
    x = _ffn(x.reshape(n, D_MODEL), l1_ff2_norm, l1_ff2_w_in, l1_ff2_w_out)
    return as_seq(x)
```

```python
import functools
import math

import jax
import jax.numpy as jnp
from jax import lax
from jax.experimental import pallas as pl
from jax.experimental.pallas import tpu as pltpu

D_MODEL = 1024
SB_HEADS = 16
SB_HEAD_DIM = D_MODEL // SB_HEADS
LRU_BLOCKS = 16
LRU_BLOCK_W = D_MODEL // LRU_BLOCKS
LRU_C = 8.0
CONV_W = 4
NORM_EPS = 1e-6
LOG2_E = math.log2(math.e)

V7X_LANES = 128
V7X_SUBLANES = 8
V7X_MXU_DIM = 256
V7X_VMEM_BYTES = 64 * 1024 * 1024
VMEM_UNCLAIMED_BYTES = 4 * 1024 * 1024

F32 = jnp.float32
BF16 = jnp.bfloat16

FFN_ROWS = 512
FFN_CHUNK = V7X_MXU_DIM
PROJ_ROWS = 1024
SB_QUERIES = 256
SB_KEYS = 128
SB_STEP = SB_QUERIES // SB_KEYS
SB_GROUPS = 4
SB_MASKED_LOGIT = -1e30
SB_DEAD_CARRY = 128.0
LRU_STEPS = 64


def _vmem_limit(resident_bytes):
    return int(min(resident_bytes * 3 // 2, V7X_VMEM_BYTES - VMEM_UNCLAIMED_BYTES))


def _dot(a, b):
    return jnp.dot(a, b, preferred_element_type=F32)


def _rms_norm_rows(x, gain_row):
    ms = jnp.mean(x * x, axis=-1, keepdims=True)
    return x * lax.rsqrt(ms + NORM_EPS) * gain_row


def _split_bf16(v):
    hi = v.astype(BF16)
    lo = (v - hi.astype(F32)).astype(BF16)
    return hi, lo


def _resident(shape):
    zeros = (0,) * len(shape)
    return pl.BlockSpec(shape, lambda *_: zeros, pipeline_mode=pl.Buffered(1))


def _ffn_kernel(x_ref, g_ref, win_ref, wout_ref, *rest, d_ff):
    x = x_ref[...]
    if len(rest) == 3:
        a_ref, wmix_ref, o_ref = rest
        x = x + _dot(a_ref[...], wmix_ref[...].astype(BF16))
    else:
        o_ref, = rest
    xn = _rms_norm_rows(x, g_ref[...]).astype(BF16)
    y = jnp.zeros_like(x)
    for c in range(d_ff // FFN_CHUNK):
        lo = c * FFN_CHUNK
        gate = _dot(xn, win_ref[:, lo:lo + FFN_CHUNK].astype(BF16))
        up = _dot(xn, win_ref[:, d_ff + lo:d_ff + lo + FFN_CHUNK].astype(BF16))
        h = (gate * jax.nn.sigmoid(gate) * up).astype(BF16)
        y = y + _dot(h, wout_ref[lo:lo + FFN_CHUNK, :].astype(BF16))
    o_ref[...] = x + 0.5 * y


def _ffn(x, gain, w_in, w_out, mix=None):
    n = x.shape[0]
    d_ff = w_out.shape[0]
    assert d_ff % FFN_CHUNK == 0 and n % FFN_ROWS == 0
    row = pl.BlockSpec((FFN_ROWS, D_MODEL), lambda i: (i, 0))
    operands = [x, gain.reshape(1, D_MODEL), w_in, w_out]
    in_specs = [row, _resident((1, D_MODEL)), _resident(w_in.shape), _resident(w_out.shape)]
    resident = 4 * (w_in.size + w_out.size) + 4 * 4 * FFN_ROWS * D_MODEL \
        + 4 * FFN_ROWS * (3 * D_MODEL + 3 * FFN_CHUNK)
    if mix is not None:
        a, w_mix = mix
        operands += [a, w_mix]
        in_specs += [row, _resident(w_mix.shape)]
        resident += 4 * w_mix.size + 2 * 2 * FFN_ROWS * D_MODEL
    return pl.pallas_call(
        functools.partial(_ffn_kernel, d_ff=d_ff),
        grid=(n // FFN_ROWS,),
        in_specs=in_specs,
        out_specs=row,
        out_shape=jax.ShapeDtypeStruct((n, D_MODEL), F32),
        compiler_params=pltpu.CompilerParams(
            dimension_semantics=("arbitrary",), vmem_limit_bytes=_vmem_limit(resident)),
        name="ffn",
    )(*operands)


def _head_rms_norm(t, mean_ref, gain_row):
    outs = []
    for c in range(D_MODEL // V7X_MXU_DIM):
        tc = t[:, c * V7X_MXU_DIM:(c + 1) * V7X_MXU_DIM]
        hi, lo = _split_bf16(tc * tc)
        ms = _dot(hi, mean_ref[...]) + _dot(lo, mean_ref[...])
        outs.append(tc * lax.rsqrt(ms + NORM_EPS))
    return jnp.concatenate(outs, axis=-1) * gain_row


def _qkv_kernel(x_ref, g_ref, w_ref, mean_ref, qg_ref, kg_ref, q_ref, k_ref, v_ref):
    xn = _rms_norm_rows(x_ref[...], g_ref[...]).astype(BF16)
    q = _dot(xn, w_ref[:, 0:D_MODEL].astype(BF16))
    q_ref[...] = (_head_rms_norm(q, mean_ref, qg_ref[...]) * (1.0 / math.sqrt(SB_HEAD_DIM))).astype(BF16)
    k = _dot(xn, w_ref[:, D_MODEL:2 * D_MODEL].astype(BF16))
    k_ref[...] = _head_rms_norm(k, mean_ref, kg_ref[...]).astype(BF16)
    v_ref[...] = _dot(xn, w_ref[:, 2 * D_MODEL:3 * D_MODEL].astype(BF16)).astype(BF16)


def _head_mean_matrix():
    r = lax.broadcasted_iota(jnp.int32, (V7X_MXU_DIM, V7X_MXU_DIM), 0) // SB_HEAD_DIM
    c = lax.broadcasted_iota(jnp.int32, (V7X_MXU_DIM, V7X_MXU_DIM), 1) // SB_HEAD_DIM
    return jnp.where(r == c, 1.0 / SB_HEAD_DIM, 0.0).astype(BF16)


def _qkv(x, gain, w_qkv, q_gain, k_gain):
    n = x.shape[0]
    row = pl.BlockSpec((PROJ_ROWS, D_MODEL), lambda i: (i, 0))
    out = jax.ShapeDtypeStruct((n, D_MODEL), BF16)
    resident = 4 * w_qkv.size + 4 * 2 * PROJ_ROWS * D_MODEL + 3 * 2 * 2 * PROJ_ROWS * D_MODEL \
        + 4 * 4 * PROJ_ROWS * D_MODEL
    tile_gain = lambda g: jnp.tile(g, SB_HEADS).reshape(1, D_MODEL)
    return pl.pallas_call(
        _qkv_kernel,
        grid=(n // PROJ_ROWS,),
        in_specs=[row, _resident((1, D_MODEL)), _resident(w_qkv.shape),
                  _resident((V7X_MXU_DIM, V7X_MXU_DIM)),
                  _resident((1, D_MODEL)), _resident((1, D_MODEL))],
        out_specs=[row, row, row],
        out_shape=[out, out, out],
        compiler_params=pltpu.CompilerParams(
            dimension_semantics=("arbitrary",), vmem_limit_bytes=_vmem_limit(resident)),
        name="qkv",
    )(x, gain.reshape(1, D_MODEL), w_qkv, _head_mean_matrix(), tile_gain(q_gain), tile_gain(k_gain))


def _sb_kernel(q_ref, k_ref, v_ref, cs_ref, o_ref, kh_s, vh_s, x_s, total_s, carry_s, acc_s, *, seq):
    Q, K = SB_QUERIES, SB_KEYS
    lanes = [slice(g * V7X_LANES, (g + 1) * V7X_LANES) for g in range(SB_GROUPS)]

    lane = lax.broadcasted_iota(jnp.int32, (K, V7X_LANES), 1)
    head_lanes = [jnp.where((lane < SB_HEAD_DIM) == (h == 0), 1.0, 0.0).astype(BF16) for h in range(2)]

    def head_rows(kb, count=1):
        return pl.ds(pl.multiple_of(kb * 2 * K, 2 * K), count * 2 * K)

    def split_heads(kb, _):
        rows = pl.ds(pl.multiple_of(kb * K, K), K)
        for g in range(SB_GROUPS):
            for src, dst in ((k_ref, kh_s), (v_ref, vh_s)):
                blk = src[rows, lanes[g]]
                dst[g, head_rows(kb), :] = jnp.concatenate(
                    [blk * head_lanes[0], blk * head_lanes[1]], axis=0)
        return 0

    lax.fori_loop(0, seq // K, split_heads, 0)

    def score(qs, kbs, diagonal):
        work = [(g, i, kb) for g in range(SB_GROUPS) for i, kb in enumerate(kbs)]
        first_row = lambda i: max(SB_STEP - 1 - i, 0) * K if diagonal else 0
        logits, s_parts = [], []
        for g, i, kb in work:
            r0 = first_row(i)
            z = lax.dot_general(qs[g][r0:], kh_s[g, head_rows(kb), :], (((1,), (1,)), ((), ())),
                                preferred_element_type=F32)
            s = jnp.maximum(z, 0.0) + jnp.log(1.0 + jnp.exp2(jnp.abs(z) * -LOG2_E))
            if diagonal and i < SB_STEP:
                row = lax.broadcasted_iota(jnp.int32, (Q - r0, 2 * K), 0)
                col = lax.broadcasted_iota(jnp.int32, (Q - r0, 2 * K), 1)
                earlier = (col & (K - 1)) < row
                s = jnp.where(earlier, s, 0.0)
                z = jnp.where(earlier, z, SB_MASKED_LOGIT)
            s_hi, s_lo = _split_bf16(s)
            for h in range(2):
                s_parts.append(jnp.concatenate([s_hi[:, h * K:(h + 1) * K],
                                                s_lo[:, h * K:(h + 1) * K]], axis=1))
            logits.append(z)
        sums = _dot(jnp.concatenate(s_parts, axis=0), cs_ref[...])
        offset = 0
        for n, (g, i, kb) in enumerate(work):
            r0 = first_row(i)
            h0 = sums[offset:offset + Q - r0]
            h1 = sums[offset + Q - r0:offset + 2 * (Q - r0)]
            offset += 2 * (Q - r0)
            from_key = jnp.concatenate([h0[:, :K], h1[:, :K]], axis=1)
            total = jnp.concatenate([h0[:, K:], h1[:, K:]], axis=1)
            x_s[g, i, r0:, :] = logits[n] - from_key if i == 0 else logits[n] - from_key - later[r0:]
            if r0:
                x_s[g, i, :r0, :] = jnp.full((r0, 2 * K), SB_MASKED_LOGIT, F32)
                total = jnp.concatenate([jnp.zeros((r0, 2 * K), F32), total], axis=0)
            later = total if i == 0 else later + total
            if i == len(kbs) - 1:
                total_s[g] = later

    def fold(kbs):
        for g in range(SB_GROUPS):
            carry = carry_s[g]
            w = [jnp.exp(x_s[g, i] - carry).astype(BF16) for i in reversed(range(len(kbs)))]
            values = vh_s[g, head_rows(kbs[-1], len(kbs)), :]
            acc_s[g] += _dot(jnp.concatenate(w, axis=1), values)
            carry_s[g] = carry + total_s[g]

    def smallest_carry():
        smallest = None
        for g in range(SB_GROUPS):
            carry = carry_s[g]
            smallest = carry if smallest is None else jnp.minimum(smallest, carry)
        smallest = jnp.minimum(smallest[:, :K], smallest[:, K:])
        smallest = jnp.min(smallest.reshape(Q // V7X_SUBLANES, V7X_SUBLANES, K), axis=0)
        return jnp.min(smallest, axis=0, keepdims=True)[0, 0]

    def q_block(qi):
        q0 = pl.multiple_of(qi * Q, Q)
        qs = [q_ref[pl.ds(q0, Q), lanes[g]] for g in range(SB_GROUPS)]
        top = (qi + 1) * SB_STEP - 1
        group = lambda j: [top - j * SB_STEP - n for n in range(SB_STEP)]
        carry_s[...] = jnp.zeros_like(carry_s)
        acc_s[...] = jnp.zeros_like(acc_s)
        first = group(0) if isinstance(qi, int) else group(0) + group(1)
        score(qs, first, True)
        fold(first)

        def sweep(j):
            score(qs, group(j), False)
            fold(group(j))
            return j, smallest_carry()

        if not isinstance(qi, int):
            lax.while_loop(lambda state: (state[0] < qi) & (state[1] <= SB_DEAD_CARRY),
                           lambda state: sweep(state[0] + 1), (jnp.int32(1), smallest_carry()))

        for g in range(SB_GROUPS):
            o_ref[pl.ds(q0, Q), lanes[g]] = acc_s[g].astype(o_ref.dtype)

    q_block(0)
    lax.fori_loop(1, seq // Q, lambda qi, _: q_block(qi) or 0, 0)


def _sb_constants():
    r = lax.broadcasted_iota(jnp.int32, (2 * SB_KEYS, 2 * SB_KEYS), 0) % SB_KEYS
    c = lax.broadcasted_iota(jnp.int32, (2 * SB_KEYS, 2 * SB_KEYS), 1)
    return jnp.where((c >= SB_KEYS) | (r >= c), 1.0, 0.0).astype(BF16)


def _stick_breaking(q, k, v, *, batch, seq):
    assert seq % SB_QUERIES == 0 and SB_QUERIES == SB_STEP * SB_KEYS and 2 * SB_KEYS == V7X_MXU_DIM
    cs = _sb_constants()
    width = SB_GROUPS * V7X_LANES
    assert D_MODEL % width == 0
    group = pl.BlockSpec((None, seq, width), lambda b, g: (b, 0, g))
    per_head = pltpu.VMEM((SB_GROUPS, 2 * seq, V7X_LANES), BF16)
    scores = (SB_GROUPS, SB_QUERIES, 2 * SB_KEYS)
    resident = 2 * 4 * 2 * seq * width + 2 * cs.size + 2 * 2 * 2 * seq * width \
        + 4 * 8 * SB_GROUPS * SB_STEP * SB_QUERIES * 2 * SB_KEYS
    return pl.pallas_call(
        functools.partial(_sb_kernel, seq=seq),
        grid=(batch, D_MODEL // width),
        in_specs=[group, group, group, _resident(cs.shape)],
        out_specs=group,
        out_shape=jax.ShapeDtypeStruct((batch, seq, D_MODEL), BF16),
        scratch_shapes=[per_head, per_head,
                        pltpu.VMEM((SB_GROUPS, 2 * SB_STEP) + scores[1:], F32),
                        pltpu.VMEM(scores, F32), pltpu.VMEM(scores, F32),
                        pltpu.VMEM((SB_GROUPS, SB_QUERIES, V7X_LANES), F32)],
        compiler_params=pltpu.CompilerParams(
            dimension_semantics=("arbitrary", "arbitrary"),
            vmem_limit_bytes=_vmem_limit(resident)),
        name="stick_breaking",
    )(q, k, v, cs)


def _rglru_kernel(x_ref, g_ref, win_ref, cw_ref, cb_ref, wg_ref, br_ref, bi_ref, lam_ref, wo_ref,
                  o_ref, xt_s, ot_s, xb_s, y_s, ri_s, a_s, u_s, h_s, state_s, win_s, wo_s, *, batch):
    rows = LRU_STEPS * batch
    halo = CONV_W * batch
    lane_tiles = D_MODEL // V7X_LANES

    @pl.when(pl.program_id(0) == 0)
    def _():
        xb_s[:, 0:halo, :] = jnp.zeros((xb_s.shape[0], halo, V7X_MXU_DIM), F32)
        state_s[...] = jnp.zeros_like(state_s)
        win_s[...] = win_ref[...].astype(BF16)
        wo_s[...] = wo_ref[...].astype(BF16)

    for b in range(batch):
        for l in range(lane_tiles):
            xt_s[l, pl.ds(b, LRU_STEPS, stride=batch), :] = \
                x_ref[b, :, l * V7X_LANES:(l + 1) * V7X_LANES]
    x = jnp.concatenate([xt_s[l] for l in range(lane_tiles)], axis=1)
    xn = _rms_norm_rows(x, g_ref[...]).astype(BF16)
    lam = lam_ref[...]
    log_sig_lam = jnp.minimum(lam, 0.0) - jnp.log1p(jnp.exp(-jnp.abs(lam)))

    for g in range(D_MODEL // V7X_MXU_DIM):
        cols = slice(g * V7X_MXU_DIM, (g + 1) * V7X_MXU_DIM)
        xb_s[g, halo:halo + rows, :] = _dot(xn, win_s[:, cols])
        y_s[g] = _dot(xn, win_s[:, D_MODEL + cols.start:D_MODEL + cols.stop])
        xc = cb_ref[:, cols] + cw_ref[CONV_W - 1:CONV_W, cols] * xb_s[g, halo:halo + rows, :]
        for j in range(CONV_W - 1):
            off = halo - (CONV_W - 1 - j) * batch
            xc = xc + cw_ref[j:j + 1, cols] * xb_s[g, off:off + rows, :]
        u_s[g] = xc
        ri_s[g] = _dot(xc.astype(BF16), wg_ref[g])
        r = jax.nn.sigmoid(ri_s[g, :, :V7X_MXU_DIM] + br_ref[:, cols])
        i = jax.nn.sigmoid(ri_s[g, :, V7X_MXU_DIM:] + bi_ref[:, cols])
        log_a = (LRU_C * r) * log_sig_lam[:, cols]
        a_s[g] = jnp.exp(log_a)
        th = jnp.tanh(log_a)
        u_s[g] = jnp.sqrt(-2.0 * th / (1.0 - th)) * (i * u_s[g])
    xb_s[:, 0:halo, :] = xb_s[:, rows:rows + halo, :]

    def tiles(ref, sl):
        return jnp.concatenate([ref[g, sl, :] for g in range(ref.shape[0])], axis=1)

    def step(t, h):
        sl = pl.ds(pl.multiple_of(t * batch, batch), batch)
        h = tiles(a_s, sl) * h + tiles(u_s, sl)
        h_s[sl, :] = h
        return h

    state_s[...] = lax.fori_loop(0, LRU_STEPS, step, state_s[...], unroll=8)
    y = jax.nn.gelu(tiles(y_s, slice(None)), approximate=True)
    out = x + _dot((h_s[...] * y).astype(BF16), wo_s[...])
    for l in range(lane_tiles):
        ot_s[l] = out[:, l * V7X_LANES:(l + 1) * V7X_LANES]
    for b in range(batch):
        for l in range(lane_tiles):
            o_ref[b, :, l * V7X_LANES:(l + 1) * V7X_LANES] = \
                ot_s[l, pl.ds(b, LRU_STEPS, stride=batch), :]


def _block_diag_tiles(w):
    per = V7X_MXU_DIM // LRU_BLOCK_W
    w = w.reshape(LRU_BLOCKS // per, per, LRU_BLOCK_W, LRU_BLOCK_W)
    eye = jnp.eye(per, dtype=w.dtype)
    return jnp.einsum("gpcd,pq->gpcqd", w, eye).reshape(LRU_BLOCKS // per, V7X_MXU_DIM, V7X_MXU_DIM)


def _rglru(x, gain, w_in, conv_w, conv_b, w_r, b_r, w_i, b_i, lam, w_o):
    batch, seq, _ = x.shape
    assert batch == V7X_SUBLANES and seq % LRU_STEPS == 0
    rows = LRU_STEPS * batch
    w_gate = jnp.concatenate([_block_diag_tiles(w_r), _block_diag_tiles(w_i)], axis=-1).astype(BF16)
    vec = lambda p: p.reshape(1, D_MODEL)
    chunk = pl.BlockSpec((batch, LRU_STEPS, D_MODEL), lambda i: (0, i, 0))
    slabs = pltpu.VMEM((D_MODEL // V7X_LANES, rows, V7X_LANES), F32)
    tile_major = lambda r: pltpu.VMEM((D_MODEL // V7X_MXU_DIM, r, V7X_MXU_DIM), F32)
    resident = 6 * (w_in.size + w_o.size) + 2 * w_gate.size + 4 * 4 * rows * D_MODEL \
        + 4 * 9 * (rows + CONV_W * batch) * D_MODEL + 4 * 6 * rows * D_MODEL
    return pl.pallas_call(
        functools.partial(_rglru_kernel, batch=batch),
        grid=(seq // LRU_STEPS,),
        in_specs=[chunk, _resident((1, D_MODEL)), _resident(w_in.shape),
                  _resident((CONV_W, D_MODEL)), _resident((1, D_MODEL)),
                  _resident(w_gate.shape), _resident((1, D_MODEL)), _resident((1, D_MODEL)),
                  _resident((1, D_MODEL)), _resident(w_o.shape)],
        out_specs=chunk,
        out_shape=jax.ShapeDtypeStruct(x.shape, F32),
        scratch_shapes=[slabs, slabs,
                        tile_major(rows + CONV_W * batch),
                        tile_major(rows),
                        pltpu.VMEM((D_MODEL // V7X_MXU_DIM, rows, 2 * V7X_MXU_DIM), F32),
                        tile_major(rows), tile_major(rows),
                        pltpu.VMEM((rows, D_MODEL), F32),
                        pltpu.VMEM((batch, D_MODEL), F32),
                        pltpu.VMEM(w_in.shape, BF16), pltpu.VMEM(w_o.shape, BF16)],
        compiler_params=pltpu.CompilerParams(
            dimension_semantics=("arbitrary",), vmem_limit_bytes=_vmem_limit(resident)),
        name="rglru",
    )(x, vec(gain), w_in, conv_w, vec(conv_b), w_gate, vec(b_r), vec(b_i), vec(lam), w_o)


def kernel(x, l0_ff1_norm, l0_ff1_w_in, l0_ff1_w_out, l0_mix_norm, l0_sb_w_qkv, l0_sb_q_norm, l0_sb_k_norm, l0_sb_w_o, l0_ff2_norm, l0_ff2_w_in, l0_ff2_w_out, l1_ff1_norm, l1_ff1_w_in, l1_ff1_w_out, l1_mix_norm, l1_lru_w_in, l1_lru_conv_w, l1_lru_conv_b, l1_lru_w_r, l1_lru_b_r, l1_lru_w_i, l1_lru_b_i, l1_lru_lambda, l1_lru_w_o, l1_ff2_norm, l1_ff2_w_in, l1_ff2_w_out):
    batch, seq, d = x.shape
    assert d == D_MODEL
    n = batch * seq
    as_seq = lambda t: t.reshape(batch, seq, D_MODEL)
    x = x.reshape(n, D_MODEL)

    x = _ffn(x, l0_ff1_norm, l0_ff1_w_in, l0_ff1_w_out)
    q, k, v = _qkv(x, l0_mix_norm, l0_sb_w_qkv, l0_sb_q_norm, l0_sb_k_norm)
    o = _stick_breaking(as_seq(q), as_seq(k), as_seq(v), batch=batch, seq=seq)
    x = _ffn(x, l0_ff2_norm, l0_ff2_w_in, l0_ff2_w_out, mix=(o.reshape(n, D_MODEL), l0_sb_w_o))

    x = _ffn(x, l1_ff1_norm, l1_ff1_w_in, l1_ff1_w_out)
    x = _rglru(as_seq(x), l1_mix_norm, l1_lru_w_in, l1_lru_conv_w, l1_lru_conv_b,
               l1_lru_w_r, l1_lru_b_r, l1_lru_w_i, l1_lru_b_i, l1_lru_lambda, l1_lru_w_o)
    x = _ffn(x.reshape(n, D_MODEL), l1_ff2_norm, l1_ff2_w_in, l1_ff2_w_out)
    return as_seq(x)
```

```python
import collections
import functools
import math

import jax
import jax.numpy as jnp
from jax import lax
from jax.experimental import pallas as pl
from jax.experimental.pallas import tpu as pltpu

D_MODEL = 1024
SB_HEADS = 16
SB_HEAD_DIM = D_MODEL // SB_HEADS
LRU_BLOCKS = 16
LRU_BLOCK_W = D_MODEL // LRU_BLOCKS
LRU_C = 8.0
CONV_W = 4
NORM_EPS = 1e-6
LOG2_E = math.log2(math.e)

V7X_LANES = 128
V7X_SUBLANES = 8
V7X_MXU_DIM = 256
V7X_VMEM_BYTES = 64 * 1024 * 1024
VMEM_UNCLAIMED_BYTES = 4 * 1024 * 1024

F32 = jnp.float32
BF16 = jnp.bfloat16

FFN_ROWS = 512
FFN_CHUNK = V7X_MXU_DIM
PROJ_ROWS = 1024
SB_QUERIES = 256
SB_KEYS = 128
SB_STEP = SB_QUERIES // SB_KEYS
SB_GROUPS = 4
SB_TOGETHER = 2
SB_MASKED_LOGIT = -1e30
SB_DEAD_CARRY = 128.0
LRU_STEPS = 64


def _vmem_limit(resident_bytes):
    return int(min(resident_bytes * 3 // 2, V7X_VMEM_BYTES - VMEM_UNCLAIMED_BYTES))


def _dot(a, b):
    return jnp.dot(a, b, preferred_element_type=F32)


def _rms_norm_rows(x, gain_row):
    ms = jnp.mean(x * x, axis=-1, keepdims=True)
    return x * lax.rsqrt(ms + NORM_EPS) * gain_row


def _split_bf16(v):
    hi = v.astype(BF16)
    lo = (v - hi.astype(F32)).astype(BF16)
    return hi, lo


def _resident(shape):
    zeros = (0,) * len(shape)
    return pl.BlockSpec(shape, lambda *_: zeros, pipeline_mode=pl.Buffered(1))


def _ffn_kernel(x_ref, g_ref, win_ref, wout_ref, *rest, d_ff):
    x = x_ref[...]
    if len(rest) == 3:
        a_ref, wmix_ref, o_ref = rest
        x = x + _dot(a_ref[...], wmix_ref[...].astype(BF16))
    else:
        o_ref, = rest
    xn = _rms_norm_rows(x, g_ref[...]).astype(BF16)
    y = jnp.zeros_like(x)
    for c in range(d_ff // FFN_CHUNK):
        lo = c * FFN_CHUNK
        gate = _dot(xn, win_ref[:, lo:lo + FFN_CHUNK].astype(BF16))
        up = _dot(xn, win_ref[:, d_ff + lo:d_ff + lo + FFN_CHUNK].astype(BF16))
        h = (gate * jax.nn.sigmoid(gate) * up).astype(BF16)
        y = y + _dot(h, wout_ref[lo:lo + FFN_CHUNK, :].astype(BF16))
    o_ref[...] = x + 0.5 * y


def _ffn(x, gain, w_in, w_out, mix=None):
    n = x.shape[0]
    d_ff = w_out.shape[0]
    assert d_ff % FFN_CHUNK == 0 and n % FFN_ROWS == 0
    row = pl.BlockSpec((FFN_ROWS, D_MODEL), lambda i: (i, 0))
    operands = [x, gain.reshape(1, D_MODEL), w_in, w_out]
    in_specs = [row, _resident((1, D_MODEL)), _resident(w_in.shape), _resident(w_out.shape)]
    resident = 4 * (w_in.size + w_out.size) + 4 * 4 * FFN_ROWS * D_MODEL \
        + 4 * FFN_ROWS * (3 * D_MODEL + 3 * FFN_CHUNK)
    if mix is not None:
        a, w_mix = mix
        operands += [a, w_mix]
        in_specs += [row, _resident(w_mix.shape)]
        resident += 4 * w_mix.size + 2 * 2 * FFN_ROWS * D_MODEL
    return pl.pallas_call(
        functools.partial(_ffn_kernel, d_ff=d_ff),
        grid=(n // FFN_ROWS,),
        in_specs=in_specs,
        out_specs=row,
        out_shape=jax.ShapeDtypeStruct((n, D_MODEL), F32),
        compiler_params=pltpu.CompilerParams(
            dimension_semantics=("arbitrary",), vmem_limit_bytes=_vmem_limit(resident)),
        name="ffn",
    )(*operands)


def _head_rms_norm(t, mean_ref, gain_row):
    outs = []
    for c in range(D_MODEL // V7X_MXU_DIM):
        tc = t[:, c * V7X_MXU_DIM:(c + 1) * V7X_MXU_DIM]
        hi, lo = _split_bf16(tc * tc)
        ms = _dot(hi, mean_ref[...]) + _dot(lo, mean_ref[...])
        outs.append(tc * lax.rsqrt(ms + NORM_EPS))
    return jnp.concatenate(outs, axis=-1) * gain_row


def _qkv_kernel(x_ref, g_ref, w_ref, mean_ref, qg_ref, kg_ref, q_ref, k_ref, v_ref):
    xn = _rms_norm_rows(x_ref[...], g_ref[...]).astype(BF16)
    q = _dot(xn, w_ref[:, 0:D_MODEL].astype(BF16))
    q_ref[...] = (_head_rms_norm(q, mean_ref, qg_ref[...]) * (1.0 / math.sqrt(SB_HEAD_DIM))).astype(BF16)
    k = _dot(xn, w_ref[:, D_MODEL:2 * D_MODEL].astype(BF16))
    k_ref[...] = _head_rms_norm(k, mean_ref, kg_ref[...]).astype(BF16)
    v_ref[...] = _dot(xn, w_ref[:, 2 * D_MODEL:3 * D_MODEL].astype(BF16)).astype(BF16)


def _head_mean_matrix():
    r = lax.broadcasted_iota(jnp.int32, (V7X_MXU_DIM, V7X_MXU_DIM), 0) // SB_HEAD_DIM
    c = lax.broadcasted_iota(jnp.int32, (V7X_MXU_DIM, V7X_MXU_DIM), 1) // SB_HEAD_DIM
    return jnp.where(r == c, 1.0 / SB_HEAD_DIM, 0.0).astype(BF16)


def _qkv(x, gain, w_qkv, q_gain, k_gain):
    n = x.shape[0]
    row = pl.BlockSpec((PROJ_ROWS, D_MODEL), lambda i: (i, 0))
    out = jax.ShapeDtypeStruct((n, D_MODEL), BF16)
    resident = 4 * w_qkv.size + 4 * 2 * PROJ_ROWS * D_MODEL + 3 * 2 * 2 * PROJ_ROWS * D_MODEL \
        + 4 * 4 * PROJ_ROWS * D_MODEL
    tile_gain = lambda g: jnp.tile(g, SB_HEADS).reshape(1, D_MODEL)
    return pl.pallas_call(
        _qkv_kernel,
        grid=(n // PROJ_ROWS,),
        in_specs=[row, _resident((1, D_MODEL)), _resident(w_qkv.shape),
                  _resident((V7X_MXU_DIM, V7X_MXU_DIM)),
                  _resident((1, D_MODEL)), _resident((1, D_MODEL))],
        out_specs=[row, row, row],
        out_shape=[out, out, out],
        compiler_params=pltpu.CompilerParams(
            dimension_semantics=("arbitrary",), vmem_limit_bytes=_vmem_limit(resident)),
        name="qkv",
    )(x, gain.reshape(1, D_MODEL), w_qkv, _head_mean_matrix(), tile_gain(q_gain), tile_gain(k_gain))


SweepState = collections.namedtuple("SweepState", "x total carry acc")


def _sb_kernel(q_ref, k_ref, v_ref, cs_ref, o_ref, kh_s, vh_s, x_s, total_s, carry_s, acc_s, *, seq):
    Q, K = SB_QUERIES, SB_KEYS
    lanes = [slice(g * V7X_LANES, (g + 1) * V7X_LANES) for g in range(SB_GROUPS)]

    lane = lax.broadcasted_iota(jnp.int32, (K, V7X_LANES), 1)
    head_lanes = [jnp.where((lane < SB_HEAD_DIM) == (h == 0), 1.0, 0.0).astype(BF16) for h in range(2)]

    def head_rows(kb, count=1):
        return pl.ds(pl.multiple_of(kb * 2 * K, 2 * K), count * 2 * K)

    def split_heads(kb, _):
        rows = pl.ds(pl.multiple_of(kb * K, K), K)
        for g in range(SB_GROUPS):
            for src, dst in ((k_ref, kh_s), (v_ref, vh_s)):
                blk = src[rows, lanes[g]]
                dst[g, head_rows(kb), :] = jnp.concatenate(
                    [blk * head_lanes[0], blk * head_lanes[1]], axis=0)
        return 0

    lax.fori_loop(0, seq // K, split_heads, 0)

    def score(st, qs, kbs, diagonal):
        work = [(g, i, kb) for g in range(SB_GROUPS) for i, kb in enumerate(kbs)]
        first_row = lambda i: max(SB_STEP - 1 - i, 0) * K if diagonal else 0
        logits, s_parts = [], []
        for g, i, kb in work:
            r0 = first_row(i)
            z = lax.dot_general(qs[g][r0:], kh_s[g, head_rows(kb), :], (((1,), (1,)), ((), ())),
                                preferred_element_type=F32)
            s = jnp.maximum(z, 0.0) + jnp.log(1.0 + jnp.exp2(jnp.abs(z) * -LOG2_E))
            if diagonal and i < SB_STEP:
                row = lax.broadcasted_iota(jnp.int32, (Q - r0, 2 * K), 0)
                col = lax.broadcasted_iota(jnp.int32, (Q - r0, 2 * K), 1)
                earlier = (col & (K - 1)) < row
                s = jnp.where(earlier, s, 0.0)
                z = jnp.where(earlier, z, SB_MASKED_LOGIT)
            s_hi, s_lo = _split_bf16(s)
            for h in range(2):
                s_parts.append(jnp.concatenate([s_hi[:, h * K:(h + 1) * K],
                                                s_lo[:, h * K:(h + 1) * K]], axis=1))
            logits.append(z)
        sums = _dot(jnp.concatenate(s_parts, axis=0), cs_ref[...])
        offset = 0
        for n, (g, i, kb) in enumerate(work):
            r0 = first_row(i)
            h0 = sums[offset:offset + Q - r0]
            h1 = sums[offset + Q - r0:offset + 2 * (Q - r0)]
            offset += 2 * (Q - r0)
            from_key = jnp.concatenate([h0[:, :K], h1[:, :K]], axis=1)
            total = jnp.concatenate([h0[:, K:], h1[:, K:]], axis=1)
            st.x[g, i, r0:, :] = logits[n] - from_key if i == 0 else logits[n] - from_key - later[r0:]
            if r0:
                st.x[g, i, :r0, :] = jnp.full((r0, 2 * K), SB_MASKED_LOGIT, F32)
                total = jnp.concatenate([jnp.zeros((r0, 2 * K), F32), total], axis=0)
            later = total if i == 0 else later + total
            if i == len(kbs) - 1:
                st.total[g] = later

    def fold(st, kbs):
        for g in range(SB_GROUPS):
            carry = st.carry[g]
            w = [jnp.exp(st.x[g, i] - carry).astype(BF16) for i in reversed(range(len(kbs)))]
            values = vh_s[g, head_rows(kbs[-1], len(kbs)), :]
            st.acc[g] += _dot(jnp.concatenate(w, axis=1), values)
            st.carry[g] = carry + st.total[g]

    def smallest_carry(st):
        smallest = None
        for g in range(SB_GROUPS):
            carry = st.carry[g]
            smallest = carry if smallest is None else jnp.minimum(smallest, carry)
        smallest = jnp.minimum(smallest[:, :K], smallest[:, K:])
        smallest = jnp.min(smallest.reshape(Q // V7X_SUBLANES, V7X_SUBLANES, K), axis=0)
        return jnp.min(smallest, axis=0, keepdims=True)[0, 0]

    def group(qi, j):
        top = (qi + 1) * SB_STEP - 1
        return [top - j * SB_STEP - n for n in range(SB_STEP)]

    def q_blocks(qis):
        slots = [SweepState(x_s.at[n], total_s.at[n], carry_s.at[n], acc_s.at[n])
                 for n in range(len(qis))]
        rows = [pl.ds(pl.multiple_of(qi * Q, Q), Q) for qi in qis]
        queries = [[q_ref[r, lanes[g]] for g in range(SB_GROUPS)] for r in rows]
        firsts = [group(qi, 0) if isinstance(qi, int) and qi == 0 else group(qi, 0) + group(qi, 1)
                  for qi in qis]
        for st, qs, first in zip(slots, queries, firsts):
            st.carry[...] = jnp.zeros_like(st.carry)
            st.acc[...] = jnp.zeros_like(st.acc)
            score(st, qs, first, True)
        for st, first in zip(slots, firsts):
            fold(st, first)
        alive = [smallest_carry(st) for st in slots]
        for st, qi, qs, r, first, smallest in zip(slots, qis, queries, rows, firsts, alive):
            if len(first) > SB_STEP:
                def sweep(state, st=st, qi=qi, qs=qs):
                    j = state[0] + 1
                    score(st, qs, group(qi, j), False)
                    fold(st, group(qi, j))
                    return j, smallest_carry(st)

                lax.while_loop(lambda state, qi=qi: (state[0] < qi) & (state[1] <= SB_DEAD_CARRY),
                               sweep, (jnp.int32(1), smallest))
            for g in range(SB_GROUPS):
                o_ref[r, lanes[g]] = st.acc[g].astype(o_ref.dtype)

    n_q = seq // Q
    trips = (n_q - 1) // SB_TOGETHER
    q_blocks([0])
    lax.fori_loop(0, trips,
                  lambda m, _: q_blocks([1 + m * SB_TOGETHER + n for n in range(SB_TOGETHER)]) or 0, 0)
    for qi in range(1 + trips * SB_TOGETHER, n_q):
        q_blocks([qi])


def _sb_constants():
    r = lax.broadcasted_iota(jnp.int32, (2 * SB_KEYS, 2 * SB_KEYS), 0) % SB_KEYS
    c = lax.broadcasted_iota(jnp.int32, (2 * SB_KEYS, 2 * SB_KEYS), 1)
    return jnp.where((c >= SB_KEYS) | (r >= c), 1.0, 0.0).astype(BF16)


def _stick_breaking(q, k, v, *, batch, seq):
    assert seq % SB_QUERIES == 0 and SB_QUERIES == SB_STEP * SB_KEYS and 2 * SB_KEYS == V7X_MXU_DIM
    cs = _sb_constants()
    width = SB_GROUPS * V7X_LANES
    assert D_MODEL % width == 0
    group = pl.BlockSpec((None, seq, width), lambda b, g: (b, 0, g))
    per_head = pltpu.VMEM((SB_GROUPS, 2 * seq, V7X_LANES), BF16)
    scores = (SB_GROUPS, SB_QUERIES, 2 * SB_KEYS)
    resident = 2 * 4 * 2 * seq * width + 2 * cs.size + 2 * 2 * 2 * seq * width \
        + 4 * 8 * SB_TOGETHER * SB_GROUPS * SB_STEP * SB_QUERIES * 2 * SB_KEYS
    return pl.pallas_call(
        functools.partial(_sb_kernel, seq=seq),
        grid=(batch, D_MODEL // width),
        in_specs=[group, group, group, _resident(cs.shape)],
        out_specs=group,
        out_shape=jax.ShapeDtypeStruct((batch, seq, D_MODEL), BF16),
        scratch_shapes=[per_head, per_head,
                        pltpu.VMEM((SB_TOGETHER, SB_GROUPS, 2 * SB_STEP) + scores[1:], F32),
                        pltpu.VMEM((SB_TOGETHER,) + scores, F32),
                        pltpu.VMEM((SB_TOGETHER,) + scores, F32),
                        pltpu.VMEM((SB_TOGETHER, SB_GROUPS, SB_QUERIES, V7X_LANES), F32)],
        compiler_params=pltpu.CompilerParams(
            dimension_semantics=("arbitrary", "arbitrary"),
            vmem_limit_bytes=_vmem_limit(resident)),
        name="stick_breaking",
    )(q, k, v, cs)


def _rglru_kernel(x_ref, g_ref, win_ref, cw_ref, cb_ref, wg_ref, br_ref, bi_ref, lam_ref, wo_ref,
                  o_ref, xt_s, ot_s, xb_s, y_s, ri_s, a_s, u_s, h_s, state_s, win_s, wo_s, *, batch):
    rows = LRU_STEPS * batch
    halo = CONV_W * batch
    lane_tiles = D_MODEL // V7X_LANES

    @pl.when(pl.program_id(0) == 0)
    def _():
        xb_s[:, 0:halo, :] = jnp.zeros((xb_s.shape[0], halo, V7X_MXU_DIM), F32)
        state_s[...] = jnp.zeros_like(state_s)
        win_s[...] = win_ref[...].astype(BF16)
        wo_s[...] = wo_ref[...].astype(BF16)

    for b in range(batch):
        for l in range(lane_tiles):
            xt_s[l, pl.ds(b, LRU_STEPS, stride=batch), :] = \
                x_ref[b, :, l * V7X_LANES:(l + 1) * V7X_LANES]
    x = jnp.concatenate([xt_s[l] for l in range(lane_tiles)], axis=1)
    xn = _rms_norm_rows(x, g_ref[...]).astype(BF16)
    lam = lam_ref[...]
    log_sig_lam = jnp.minimum(lam, 0.0) - jnp.log1p(jnp.exp(-jnp.abs(lam)))

    for g in range(D_MODEL // V7X_MXU_DIM):
        cols = slice(g * V7X_MXU_DIM, (g + 1) * V7X_MXU_DIM)
        xb_s[g, halo:halo + rows, :] = _dot(xn, win_s[:, cols])
        y_s[g] = _dot(xn, win_s[:, D_MODEL + cols.start:D_MODEL + cols.stop])
        xc = cb_ref[:, cols] + cw_ref[CONV_W - 1:CONV_W, cols] * xb_s[g, halo:halo + rows, :]
        for j in range(CONV_W - 1):
            off = halo - (CONV_W - 1 - j) * batch
            xc = xc + cw_ref[j:j + 1, cols] * xb_s[g, off:off + rows, :]
        u_s[g] = xc
        ri_s[g] = _dot(xc.astype(BF16), wg_ref[g])
        r = jax.nn.sigmoid(ri_s[g, :, :V7X_MXU_DIM] + br_ref[:, cols])
        i = jax.nn.sigmoid(ri_s[g, :, V7X_MXU_DIM:] + bi_ref[:, cols])
        log_a = (LRU_C * r) * log_sig_lam[:, cols]
        a_s[g] = jnp.exp(log_a)
        th = jnp.tanh(log_a)
        u_s[g] = jnp.sqrt(-2.0 * th / (1.0 - th)) * (i * u_s[g])
    xb_s[:, 0:halo, :] = xb_s[:, rows:rows + halo, :]

    def tiles(ref, sl):
        return jnp.concatenate([ref[g, sl, :] for g in range(ref.shape[0])], axis=1)

    def step(t, h):
        sl = pl.ds(pl.multiple_of(t * batch, batch), batch)
        h = tiles(a_s, sl) * h + tiles(u_s, sl)
        h_s[sl, :] = h
        return h

    state_s[...] = lax.fori_loop(0, LRU_STEPS, step, state_s[...], unroll=8)
    y = jax.nn.gelu(tiles(y_s, slice(None)), approximate=True)
    out = x + _dot((h_s[...] * y).astype(BF16), wo_s[...])
    for l in range(lane_tiles):
        ot_s[l] = out[:, l * V7X_LANES:(l + 1) * V7X_LANES]
    for b in range(batch):
        for l in range(lane_tiles):
            o_ref[b, :, l * V7X_LANES:(l + 1) * V7X_LANES] = \
                ot_s[l, pl.ds(b, LRU_STEPS, stride=batch), :]


def _block_diag_tiles(w):
    per = V7X_MXU_DIM // LRU_BLOCK_W
    w = w.reshape(LRU_BLOCKS // per, per, LRU_BLOCK_W, LRU_BLOCK_W)
    eye = jnp.eye(per, dtype=w.dtype)
    return jnp.einsum("gpcd,pq->gpcqd", w, eye).reshape(LRU_BLOCKS // per, V7X_MXU_DIM, V7X_MXU_DIM)


def _rglru(x, gain, w_in, conv_w, conv_b, w_r, b_r, w_i, b_i, lam, w_o):
    batch, seq, _ = x.shape
    assert batch == V7X_SUBLANES and seq % LRU_STEPS == 0
    rows = LRU_STEPS * batch
    w_gate = jnp.concatenate([_block_diag_tiles(w_r), _block_diag_tiles(w_i)], axis=-1).astype(BF16)
    vec = lambda p: p.reshape(1, D_MODEL)
    chunk = pl.BlockSpec((batch, LRU_STEPS, D_MODEL), lambda i: (0, i, 0))
    slabs = pltpu.VMEM((D_MODEL // V7X_LANES, rows, V7X_LANES), F32)
    tile_major = lambda r: pltpu.VMEM((D_MODEL // V7X_MXU_DIM, r, V7X_MXU_DIM), F32)
    resident = 6 * (w_in.size + w_o.size) + 2 * w_gate.size + 4 * 4 * rows * D_MODEL \
        + 4 * 9 * (rows + CONV_W * batch) * D_MODEL + 4 * 6 * rows * D_MODEL
    return pl.pallas_call(
        functools.partial(_rglru_kernel, batch=batch),
        grid=(seq // LRU_STEPS,),
        in_specs=[chunk, _resident((1, D_MODEL)), _resident(w_in.shape),
                  _resident((CONV_W, D_MODEL)), _resident((1, D_MODEL)),
                  _resident(w_gate.shape), _resident((1, D_MODEL)), _resident((1, D_MODEL)),
                  _resident((1, D_MODEL)), _resident(w_o.shape)],
        out_specs=chunk,
        out_shape=jax.ShapeDtypeStruct(x.shape, F32),
        scratch_shapes=[slabs, slabs,
                        tile_major(rows + CONV_W * batch),
                        tile_major(rows),
                        pltpu.VMEM((D_MODEL // V7X_MXU_DIM, rows, 2 * V7X_MXU_DIM), F32),
                        tile_major(rows), tile_major(rows),
                        pltpu.VMEM((rows, D_MODEL), F32),
                        pltpu.VMEM((batch, D_MODEL), F32),
                        pltpu.VMEM(w_in.shape, BF16), pltpu.VMEM(w_o.shape, BF16)],
        compiler_params=pltpu.CompilerParams(
            dimension_semantics=("arbitrary",), vmem_limit_bytes=_vmem_limit(resident)),
        name="rglru",
    )(x, vec(gain), w_in, conv_w, vec(conv_b), w_gate, vec(b_r), vec(b_i), vec(lam), w_o)


def kernel(x, l0_ff1_norm, l0_ff1_w_in, l0_ff1_w_out, l0_mix_norm, l0_sb_w_qkv, l0_sb_q_norm, l0_sb_k_norm, l0_sb_w_o, l0_ff2_norm, l0_ff2_w_in, l0_ff2_w_out, l1_ff1_norm, l1_ff1_w_in, l1_ff1_w_out, l1_mix_norm, l1_lru_w_in, l1_lru_conv_w, l1_lru_conv_b, l1_lru_w_r, l1_lru_b_r, l1_lru_w_i, l1_lru_b_i, l1_lru_lambda, l1_lru_w_o, l1_ff2_norm, l1_ff2_w_in, l1_ff2_w_out):
    batch, seq, d = x.shape
    assert d == D_MODEL
    n = batch * seq
    as_seq = lambda t: t.reshape(batch, seq, D_MODEL)
    x = x.reshape(n, D_MODEL)

    x = _ffn(x, l0_ff1_norm, l0_ff1_w_in, l0_ff1_w_out)
    q, k, v = _qkv(x, l0_mix_norm, l0_sb_w_qkv, l0_sb_q_norm, l0_sb_k_norm)
    o = _stick_breaking(as_seq(q), as_seq(k), as_seq(v), batch=batch, seq=seq)
    x = _ffn(x, l0_ff2_norm, l0_ff2_w_in, l0_ff2_w_out, mix=(o.reshape(n, D_MODEL), l0_sb_w_o))

    x = _ffn(x, l1_ff1_norm, l1_ff1_w_in, l1_ff1_w_out)
    x = _rglru(as_seq(x), l1_mix_norm, l1_lru_w_in, l1_lru_conv_w, l1_lru_conv_b,
               l1_lru_w_r, l1_lru_b_r, l1_lru_w_i, l1_lru_b_i, l1_lru_lambda, l1_lru_w_o)
    x = _ffn(x.reshape(n, D_MODEL), l1_ff2_norm, l1_ff2_w_in, l1_ff2_w_out)
    return as_seq(x)
```

```python
import collections
import functools
import math

import jax
import jax.numpy as jnp
from jax import lax
from jax.experimental import pallas as pl
from jax.experimental.pallas import tpu as pltpu

D_MODEL = 1024
SB_HEADS = 16
SB_HEAD_DIM = D_MODEL // SB_HEADS
LRU_BLOCKS = 16
LRU_BLOCK_W = D_MODEL // LRU_BLOCKS
LRU_C = 8.0
CONV_W = 4
NORM_EPS = 1e-6
LOG2_E = math.log2(math.e)

V7X_LANES = 128
V7X_SUBLANES = 8
V7X_MXU_DIM = 256
V7X_VMEM_BYTES = 64 * 1024 * 1024
VMEM_UNCLAIMED_BYTES = 4 * 1024 * 1024

F32 = jnp.float32
BF16 = jnp.bfloat16

FFN_ROWS = 512
FFN_CHUNK = V7X_MXU_DIM
PROJ_ROWS = 1024
SB_QUERIES = 256
SB_KEYS = 128
SB_STEP = SB_QUERIES // SB_KEYS
SB_GROUPS = 4
SB_TOGETHER = 2
SB_MASKED_LOGIT = -1e30
SB_DEAD_CARRY = 128.0
LRU_STEPS = 64


def _vmem_limit(resident_bytes):
    return int(min(resident_bytes * 3 // 2, V7X_VMEM_BYTES - VMEM_UNCLAIMED_BYTES))


def _dot(a, b):
    return jnp.dot(a, b, preferred_element_type=F32)


def _rms_norm_rows(x, gain_row):
    ms = jnp.mean(x * x, axis=-1, keepdims=True)
    return x * lax.rsqrt(ms + NORM_EPS) * gain_row


def _split_bf16(v):
    hi = v.astype(BF16)
    lo = (v - hi.astype(F32)).astype(BF16)
    return hi, lo


def _resident(shape):
    zeros = (0,) * len(shape)
    return pl.BlockSpec(shape, lambda *_: zeros, pipeline_mode=pl.Buffered(1))


def _ffn_kernel(x_ref, g_ref, win_ref, wout_ref, *rest, d_ff):
    x = x_ref[...]
    if len(rest) == 3:
        a_ref, wmix_ref, o_ref = rest
        x = x + _dot(a_ref[...], wmix_ref[...].astype(BF16))
    else:
        o_ref, = rest
    xn = _rms_norm_rows(x, g_ref[...]).astype(BF16)
    y = jnp.zeros_like(x)
    for c in range(d_ff // FFN_CHUNK):
        lo = c * FFN_CHUNK
        gate = _dot(xn, win_ref[:, lo:lo + FFN_CHUNK].astype(BF16))
        up = _dot(xn, win_ref[:, d_ff + lo:d_ff + lo + FFN_CHUNK].astype(BF16))
        h = (gate * jax.nn.sigmoid(gate) * up).astype(BF16)
        y = y + _dot(h, wout_ref[lo:lo + FFN_CHUNK, :].astype(BF16))
    o_ref[...] = x + 0.5 * y


def _ffn(x, gain, w_in, w_out, mix=None):
    n = x.shape[0]
    d_ff = w_out.shape[0]
    assert d_ff % FFN_CHUNK == 0 and n % FFN_ROWS == 0
    row = pl.BlockSpec((FFN_ROWS, D_MODEL), lambda i: (i, 0))
    operands = [x, gain.reshape(1, D_MODEL), w_in, w_out]
    in_specs = [row, _resident((1, D_MODEL)), _resident(w_in.shape), _resident(w_out.shape)]
    resident = 4 * (w_in.size + w_out.size) + 4 * 4 * FFN_ROWS * D_MODEL \
        + 4 * FFN_ROWS * (3 * D_MODEL + 3 * FFN_CHUNK)
    if mix is not None:
        a, w_mix = mix
        operands += [a, w_mix]
        in_specs += [row, _resident(w_mix.shape)]
        resident += 4 * w_mix.size + 2 * 2 * FFN_ROWS * D_MODEL
    return pl.pallas_call(
        functools.partial(_ffn_kernel, d_ff=d_ff),
        grid=(n // FFN_ROWS,),
        in_specs=in_specs,
        out_specs=row,
        out_shape=jax.ShapeDtypeStruct((n, D_MODEL), F32),
        compiler_params=pltpu.CompilerParams(
            dimension_semantics=("arbitrary",), vmem_limit_bytes=_vmem_limit(resident)),
        name="ffn",
    )(*operands)


def _head_rms_norm(t, mean_ref, gain_row):
    outs = []
    for c in range(D_MODEL // V7X_MXU_DIM):
        tc = t[:, c * V7X_MXU_DIM:(c + 1) * V7X_MXU_DIM]
        hi, lo = _split_bf16(tc * tc)
        ms = _dot(hi, mean_ref[...]) + _dot(lo, mean_ref[...])
        outs.append(tc * lax.rsqrt(ms + NORM_EPS))
    return jnp.concatenate(outs, axis=-1) * gain_row


def _qkv_kernel(x_ref, g_ref, w_ref, mean_ref, qg_ref, kg_ref, q_ref, k_ref, v_ref):
    xn = _rms_norm_rows(x_ref[...], g_ref[...]).astype(BF16)
    q = _dot(xn, w_ref[:, 0:D_MODEL].astype(BF16))
    q_ref[...] = (_head_rms_norm(q, mean_ref, qg_ref[...]) * (1.0 / math.sqrt(SB_HEAD_DIM))).astype(BF16)
    k = _dot(xn, w_ref[:, D_MODEL:2 * D_MODEL].astype(BF16))
    k_ref[...] = _head_rms_norm(k, mean_ref, kg_ref[...]).astype(BF16)
    v_ref[...] = _dot(xn, w_ref[:, 2 * D_MODEL:3 * D_MODEL].astype(BF16)).astype(BF16)


def _head_mean_matrix():
    r = lax.broadcasted_iota(jnp.int32, (V7X_MXU_DIM, V7X_MXU_DIM), 0) // SB_HEAD_DIM
    c = lax.broadcasted_iota(jnp.int32, (V7X_MXU_DIM, V7X_MXU_DIM), 1) // SB_HEAD_DIM
    return jnp.where(r == c, 1.0 / SB_HEAD_DIM, 0.0).astype(BF16)


def _qkv(x, gain, w_qkv, q_gain, k_gain):
    n = x.shape[0]
    row = pl.BlockSpec((PROJ_ROWS, D_MODEL), lambda i: (i, 0))
    out = jax.ShapeDtypeStruct((n, D_MODEL), BF16)
    resident = 4 * w_qkv.size + 4 * 2 * PROJ_ROWS * D_MODEL + 3 * 2 * 2 * PROJ_ROWS * D_MODEL \
        + 4 * 4 * PROJ_ROWS * D_MODEL
    tile_gain = lambda g: jnp.tile(g, SB_HEADS).reshape(1, D_MODEL)
    return pl.pallas_call(
        _qkv_kernel,
        grid=(n // PROJ_ROWS,),
        in_specs=[row, _resident((1, D_MODEL)), _resident(w_qkv.shape),
                  _resident((V7X_MXU_DIM, V7X_MXU_DIM)),
                  _resident((1, D_MODEL)), _resident((1, D_MODEL))],
        out_specs=[row, row, row],
        out_shape=[out, out, out],
        compiler_params=pltpu.CompilerParams(
            dimension_semantics=("arbitrary",), vmem_limit_bytes=_vmem_limit(resident)),
        name="qkv",
    )(x, gain.reshape(1, D_MODEL), w_qkv, _head_mean_matrix(), tile_gain(q_gain), tile_gain(k_gain))


SweepState = collections.namedtuple("SweepState", "x total carry acc")


def _sb_kernel(q_ref, k_ref, v_ref, cs_ref, o_ref, kh_s, vh_s, x_s, total_s, carry_s, acc_s, *, seq):
    Q, K = SB_QUERIES, SB_KEYS
    lanes = [slice(g * V7X_LANES, (g + 1) * V7X_LANES) for g in range(SB_GROUPS)]

    lane = lax.broadcasted_iota(jnp.int32, (K, V7X_LANES), 1)
    head_lanes = [jnp.where((lane < SB_HEAD_DIM) == (h == 0), 1.0, 0.0).astype(BF16) for h in range(2)]

    def head_rows(kb, count=1):
        return pl.ds(pl.multiple_of(kb * 2 * K, 2 * K), count * 2 * K)

    def split_heads(kb, _):
        rows = pl.ds(pl.multiple_of(kb * K, K), K)
        for g in range(SB_GROUPS):
            for src, dst in ((k_ref, kh_s), (v_ref, vh_s)):
                blk = src[rows, lanes[g]]
                dst[g, head_rows(kb), :] = jnp.concatenate(
                    [blk * head_lanes[0], blk * head_lanes[1]], axis=0)
        return 0

    lax.fori_loop(0, seq // K, split_heads, 0)

    def score(st, qs, kbs, diagonal):
        work = [(g, i, kb) for g in range(SB_GROUPS) for i, kb in enumerate(kbs)]
        first_row = lambda i: max(SB_STEP - 1 - i, 0) * K if diagonal else 0
        logits, s_parts = [], []
        for g, i, kb in work:
            r0 = first_row(i)
            z = lax.dot_general(qs[g][r0:], kh_s[g, head_rows(kb), :], (((1,), (1,)), ((), ())),
                                preferred_element_type=F32)
            s = jnp.maximum(z, 0.0) + jnp.log(1.0 + jnp.exp2(jnp.abs(z) * -LOG2_E))
            if diagonal and i < SB_STEP:
                row = lax.broadcasted_iota(jnp.int32, (Q - r0, 2 * K), 0)
                col = lax.broadcasted_iota(jnp.int32, (Q - r0, 2 * K), 1)
                earlier = (col & (K - 1)) < row
                s = jnp.where(earlier, s, 0.0)
                z = jnp.where(earlier, z, SB_MASKED_LOGIT)
            s_hi, s_lo = _split_bf16(s)
            for h in range(2):
                s_parts.append(jnp.concatenate([s_hi[:, h * K:(h + 1) * K],
                                                s_lo[:, h * K:(h + 1) * K]], axis=1))
            logits.append(z)
        sums = _dot(jnp.concatenate(s_parts, axis=0), cs_ref[...])
        offset = 0
        for n, (g, i, kb) in enumerate(work):
            r0 = first_row(i)
            h0 = sums[offset:offset + Q - r0]
            h1 = sums[offset + Q - r0:offset + 2 * (Q - r0)]
            offset += 2 * (Q - r0)
            from_key = jnp.concatenate([h0[:, :K], h1[:, :K]], axis=1)
            total = jnp.concatenate([h0[:, K:], h1[:, K:]], axis=1)
            st.x[g, i, r0:, :] = logits[n] - from_key if i == 0 else logits[n] - from_key - later[r0:]
            if r0:
                st.x[g, i, :r0, :] = jnp.full((r0, 2 * K), SB_MASKED_LOGIT, F32)
                total = jnp.concatenate([jnp.zeros((r0, 2 * K), F32), total], axis=0)
            later = total if i == 0 else later + total
            if i == len(kbs) - 1:
                st.total[g] = later

    def fold(st, kbs):
        for g in range(SB_GROUPS):
            carry = st.carry[g]
            w = [jnp.exp(st.x[g, i] - carry).astype(BF16) for i in reversed(range(len(kbs)))]
            values = vh_s[g, head_rows(kbs[-1], len(kbs)), :]
            st.acc[g] += _dot(jnp.concatenate(w, axis=1), values)
            st.carry[g] = carry + st.total[g]

    def smallest_carry(st):
        smallest = None
        for g in range(SB_GROUPS):
            carry = st.carry[g]
            smallest = carry if smallest is None else jnp.minimum(smallest, carry)
        smallest = jnp.minimum(smallest[:, :K], smallest[:, K:])
        smallest = jnp.min(smallest.reshape(Q // V7X_SUBLANES, V7X_SUBLANES, K), axis=0)
        return jnp.min(smallest, axis=0, keepdims=True)[0, 0]

    def group(qi, j):
        top = (qi + 1) * SB_STEP - 1
        return [top - j * SB_STEP - n for n in range(SB_STEP)]

    def q_blocks(qis):
        slots = [SweepState(x_s.at[n], total_s.at[n], carry_s.at[n], acc_s.at[n])
                 for n in range(len(qis))]
        rows = [pl.ds(pl.multiple_of(qi * Q, Q), Q) for qi in qis]
        queries = [[q_ref[r, lanes[g]] for g in range(SB_GROUPS)] for r in rows]
        firsts = [group(qi, 0) if isinstance(qi, int) and qi == 0 else group(qi, 0) + group(qi, 1)
                  for qi in qis]
        for st, qs, first in zip(slots, queries, firsts):
            st.carry[...] = jnp.zeros_like(st.carry)
            st.acc[...] = jnp.zeros_like(st.acc)
            score(st, qs, first, True)
        for st, first in zip(slots, firsts):
            fold(st, first)
        alive = [smallest_carry(st) for st in slots]
        for st, qi, qs, r, first, smallest in zip(slots, qis, queries, rows, firsts, alive):
            if len(first) > SB_STEP:
                def sweep(state, st=st, qi=qi, qs=qs):
                    j = state[0] + 1
                    score(st, qs, group(qi, j), False)
                    fold(st, group(qi, j))
                    return j, smallest_carry(st)

                lax.while_loop(lambda state, qi=qi: (state[0] < qi) & (state[1] <= SB_DEAD_CARRY),
                               sweep, (jnp.int32(1), smallest))
            for g in range(SB_GROUPS):
                o_ref[r, lanes[g]] = st.acc[g].astype(o_ref.dtype)

    n_q = seq // Q
    sets = n_q // SB_TOGETHER
    q_blocks(list(range(min(SB_TOGETHER, n_q))))
    lax.fori_loop(1, sets,
                  lambda m, _: q_blocks([m * SB_TOGETHER + n for n in range(SB_TOGETHER)]) or 0, 0)
    if n_q > SB_TOGETHER and n_q % SB_TOGETHER:
        q_blocks(list(range(sets * SB_TOGETHER, n_q)))


def _sb_constants():
    r = lax.broadcasted_iota(jnp.int32, (2 * SB_KEYS, 2 * SB_KEYS), 0) % SB_KEYS
    c = lax.broadcasted_iota(jnp.int32, (2 * SB_KEYS, 2 * SB_KEYS), 1)
    return jnp.where((c >= SB_KEYS) | (r >= c), 1.0, 0.0).astype(BF16)


def _stick_breaking(q, k, v, *, batch, seq):
    assert seq % SB_QUERIES == 0 and SB_QUERIES == SB_STEP * SB_KEYS and 2 * SB_KEYS == V7X_MXU_DIM
    cs = _sb_constants()
    width = SB_GROUPS * V7X_LANES
    assert D_MODEL % width == 0
    group = pl.BlockSpec((None, seq, width), lambda b, g: (b, 0, g))
    per_head = pltpu.VMEM((SB_GROUPS, 2 * seq, V7X_LANES), BF16)
    scores = (SB_GROUPS, SB_QUERIES, 2 * SB_KEYS)
    resident = 2 * 4 * 2 * seq * width + 2 * cs.size + 2 * 2 * 2 * seq * width \
        + 4 * 8 * SB_TOGETHER * SB_GROUPS * SB_STEP * SB_QUERIES * 2 * SB_KEYS
    return pl.pallas_call(
        functools.partial(_sb_kernel, seq=seq),
        grid=(batch, D_MODEL // width),
        in_specs=[group, group, group, _resident(cs.shape)],
        out_specs=group,
        out_shape=jax.ShapeDtypeStruct((batch, seq, D_MODEL), BF16),
        scratch_shapes=[per_head, per_head,
                        pltpu.VMEM((SB_TOGETHER, SB_GROUPS, 2 * SB_STEP) + scores[1:], F32),
                        pltpu.VMEM((SB_TOGETHER,) + scores, F32),
                        pltpu.VMEM((SB_TOGETHER,) + scores, F32),
                        pltpu.VMEM((SB_TOGETHER, SB_GROUPS, SB_QUERIES, V7X_LANES), F32)],
        compiler_params=pltpu.CompilerParams(
            dimension_semantics=("arbitrary", "arbitrary"),
            vmem_limit_bytes=_vmem_limit(resident)),
        name="stick_breaking",
    )(q, k, v, cs)


def _rglru_kernel(x_ref, g_ref, win_ref, cw_ref, cb_ref, wg_ref, br_ref, bi_ref, lam_ref, wo_ref,
                  o_ref, xt_s, ot_s, xb_s, y_s, ri_s, a_s, u_s, h_s, state_s, win_s, wo_s, *, batch):
    rows = LRU_STEPS * batch
    halo = CONV_W * batch
    lane_tiles = D_MODEL // V7X_LANES

    @pl.when(pl.program_id(0) == 0)
    def _():
        xb_s[:, 0:halo, :] = jnp.zeros((xb_s.shape[0], halo, V7X_MXU_DIM), F32)
        state_s[...] = jnp.zeros_like(state_s)
        win_s[...] = win_ref[...].astype(BF16)
        wo_s[...] = wo_ref[...].astype(BF16)

    for b in range(batch):
        for l in range(lane_tiles):
            xt_s[l, pl.ds(b, LRU_STEPS, stride=batch), :] = \
                x_ref[b, :, l * V7X_LANES:(l + 1) * V7X_LANES]
    x = jnp.concatenate([xt_s[l] for l in range(lane_tiles)], axis=1)
    xn = _rms_norm_rows(x, g_ref[...]).astype(BF16)
    lam = lam_ref[...]
    log_sig_lam = jnp.minimum(lam, 0.0) - jnp.log1p(jnp.exp(-jnp.abs(lam)))

    for g in range(D_MODEL // V7X_MXU_DIM):
        cols = slice(g * V7X_MXU_DIM, (g + 1) * V7X_MXU_DIM)
        xb_s[g, halo:halo + rows, :] = _dot(xn, win_s[:, cols])
        y_s[g] = _dot(xn, win_s[:, D_MODEL + cols.start:D_MODEL + cols.stop])
        xc = cb_ref[:, cols] + cw_ref[CONV_W - 1:CONV_W, cols] * xb_s[g, halo:halo + rows, :]
        for j in range(CONV_W - 1):
            off = halo - (CONV_W - 1 - j) * batch
            xc = xc + cw_ref[j:j + 1, cols] * xb_s[g, off:off + rows, :]
        u_s[g] = xc
        ri_s[g] = _dot(xc.astype(BF16), wg_ref[g])
        r = jax.nn.sigmoid(ri_s[g, :, :V7X_MXU_DIM] + br_ref[:, cols])
        i = jax.nn.sigmoid(ri_s[g, :, V7X_MXU_DIM:] + bi_ref[:, cols])
        log_a = (LRU_C * r) * log_sig_lam[:, cols]
        a_s[g] = jnp.exp(log_a)
        th = jnp.tanh(log_a)
        u_s[g] = jnp.sqrt(-2.0 * th / (1.0 - th)) * (i * u_s[g])
    xb_s[:, 0:halo, :] = xb_s[:, rows:rows + halo, :]

    def tiles(ref, sl):
        return jnp.concatenate([ref[g, sl, :] for g in range(ref.shape[0])], axis=1)

    def step(t, h):
        sl = pl.ds(pl.multiple_of(t * batch, batch), batch)
        h = tiles(a_s, sl) * h + tiles(u_s, sl)
        h_s[sl, :] = h
        return h

    state_s[...] = lax.fori_loop(0, LRU_STEPS, step, state_s[...], unroll=8)
    y = jax.nn.gelu(tiles(y_s, slice(None)), approximate=True)
    out = x + _dot((h_s[...] * y).astype(BF16), wo_s[...])
    for l in range(lane_tiles):
        ot_s[l] = out[:, l * V7X_LANES:(l + 1) * V7X_LANES]
    for b in range(batch):
        for l in range(lane_tiles):
            o_ref[b, :, l * V7X_LANES:(l + 1) * V7X_LANES] = \
                ot_s[l, pl.ds(b, LRU_STEPS, stride=batch), :]


def _block_diag_tiles(w):
    per = V7X_MXU_DIM // LRU_BLOCK_W
    w = w.reshape(LRU_BLOCKS // per, per, LRU_BLOCK_W, LRU_BLOCK_W)
    eye = jnp.eye(per, dtype=w.dtype)
    return jnp.einsum("gpcd,pq->gpcqd", w, eye).reshape(LRU_BLOCKS // per, V7X_MXU_DIM, V7X_MXU_DIM)


def _rglru(x, gain, w_in, conv_w, conv_b, w_r, b_r, w_i, b_i, lam, w_o):
    batch, seq, _ = x.shape
    assert batch == V7X_SUBLANES and seq % LRU_STEPS == 0
    rows = LRU_STEPS * batch
    w_gate = jnp.concatenate([_block_diag_tiles(w_r), _block_diag_tiles(w_i)], axis=-1).astype(BF16)
    vec = lambda p: p.reshape(1, D_MODEL)
    chunk = pl.BlockSpec((batch, LRU_STEPS, D_MODEL), lambda i: (0, i, 0))
    slabs = pltpu.VMEM((D_MODEL // V7X_LANES, rows, V7X_LANES), F32)
    tile_major = lambda r: pltpu.VMEM((D_MODEL // V7X_MXU_DIM, r, V7X_MXU_DIM), F32)
    resident = 6 * (w_in.size + w_o.size) + 2 * w_gate.size + 4 * 4 * rows * D_MODEL \
        + 4 * 9 * (rows + CONV_W * batch) * D_MODEL + 4 * 6 * rows * D_MODEL
    return pl.pallas_call(
        functools.partial(_rglru_kernel, batch=batch),
        grid=(seq // LRU_STEPS,),
        in_specs=[chunk, _resident((1, D_MODEL)), _resident(w_in.shape),
                  _resident((CONV_W, D_MODEL)), _resident((1, D_MODEL)),
                  _resident(w_gate.shape), _resident((1, D_MODEL)), _resident((1, D_MODEL)),
                  _resident((1, D_MODEL)), _resident(w_o.shape)],
        out_specs=chunk,
        out_shape=jax.ShapeDtypeStruct(x.shape, F32),
        scratch_shapes=[slabs, slabs,
                        tile_major(rows + CONV_W * batch),
                        tile_major(rows),
                        pltpu.VMEM((D_MODEL // V7X_MXU_DIM, rows, 2 * V7X_MXU_DIM), F32),
                        tile_major(rows), tile_major(rows),
                        pltpu.VMEM((rows, D_MODEL), F32),
                        pltpu.VMEM((batch, D_MODEL), F32),
                        pltpu.VMEM(w_in.shape, BF16), pltpu.VMEM(w_o.shape, BF16)],
        compiler_params=pltpu.CompilerParams(
            dimension_semantics=("arbitrary",), vmem_limit_bytes=_vmem_limit(resident)),
        name="rglru",
    )(x, vec(gain), w_in, conv_w, vec(conv_b), w_gate, vec(b_r), vec(b_i), vec(lam), w_o)


def kernel(x, l0_ff1_norm, l0_ff1_w_in, l0_ff1_w_out, l0_mix_norm, l0_sb_w_qkv, l0_sb_q_norm, l0_sb_k_norm, l0_sb_w_o, l0_ff2_norm, l0_ff2_w_in, l0_ff2_w_out, l1_ff1_norm, l1_ff1_w_in, l1_ff1_w_out, l1_mix_norm, l1_lru_w_in, l1_lru_conv_w, l1_lru_conv_b, l1_lru_w_r, l1_lru_b_r, l1_lru_w_i, l1_lru_b_i, l1_lru_lambda, l1_lru_w_o, l1_ff2_norm, l1_ff2_w_in, l1_ff2_w_out):
    batch, seq, d = x.shape
    assert d == D_MODEL
    n = batch * seq
    as_seq = lambda t: t.reshape(batch, seq, D_MODEL)
    x = x.reshape(n, D_MODEL)

    x = _ffn(x, l0_ff1_norm, l0_ff1_w_in, l0_ff1_w_out)
    q, k, v = _qkv(x, l0_mix_norm, l0_sb_w_qkv, l0_sb_q_norm, l0_sb_k_norm)
    o = _stick_breaking(as_seq(q), as_seq(k), as_seq(v), batch=batch, seq=seq)
    x = _ffn(x, l0_ff2_norm, l0_ff2_w_in, l0_ff2_w_out, mix=(o.reshape(n, D_MODEL), l0_sb_w_o))

    x = _ffn(x, l1_ff1_norm, l1_ff1_w_in, l1_ff1_w_out)
    x = _rglru(as_seq(x), l1_mix_norm, l1_lru_w_in, l1_lru_conv_w, l1_lru_conv_b,
               l1_lru_w_r, l1_lru_b_r, l1_lru_w_i, l1_lru_b_i, l1_lru_lambda, l1_lru_w_o)
    x = _ffn(x.reshape(n, D_MODEL), l1_ff2_norm, l1_ff2_w_in, l1_ff2_w_out)
    return as_seq(x)
```

```python
import collections
import functools
import math

import jax
import jax.numpy as jnp
from jax import lax
from jax.experimental import pallas as pl
from jax.experimental.pallas import tpu as pltpu

D_MODEL = 1024
SB_HEADS = 16
SB_HEAD_DIM = D_MODEL // SB_HEADS
LRU_BLOCKS = 16
LRU_BLOCK_W = D_MODEL // LRU_BLOCKS
LRU_C = 8.0
CONV_W = 4
NORM_EPS = 1e-6
LOG2_E = math.log2(math.e)

V7X_LANES = 128
V7X_SUBLANES = 8
V7X_MXU_DIM = 256
V7X_VMEM_BYTES = 64 * 1024 * 1024
VMEM_UNCLAIMED_BYTES = 4 * 1024 * 1024

F32 = jnp.float32
BF16 = jnp.bfloat16

FFN_ROWS = 512
FFN_CHUNK = V7X_MXU_DIM
PROJ_ROWS = 1024
SB_QUERIES = 256
SB_KEYS = 128
SB_STEP = SB_QUERIES // SB_KEYS
SB_GROUPS = 4
SB_TOGETHER = 2
SB_MASKED_LOGIT = -1e30
SB_DEAD_CARRY = 128.0
LRU_STEPS = 64


def _vmem_limit(resident_bytes):
    return int(min(resident_bytes * 3 // 2, V7X_VMEM_BYTES - VMEM_UNCLAIMED_BYTES))


def _dot(a, b):
    return jnp.dot(a, b, preferred_element_type=F32)


def _rms_norm_rows(x, gain_row):
    ms = jnp.mean(x * x, axis=-1, keepdims=True)
    return x * lax.rsqrt(ms + NORM_EPS) * gain_row


def _split_bf16(v):
    hi = v.astype(BF16)
    lo = (v - hi.astype(F32)).astype(BF16)
    return hi, lo


def _resident(shape):
    zeros = (0,) * len(shape)
    return pl.BlockSpec(shape, lambda *_: zeros, pipeline_mode=pl.Buffered(1))


def _ffn_kernel(x_ref, g_ref, win_ref, wout_ref, *rest, d_ff):
    x = x_ref[...]
    if len(rest) == 3:
        a_ref, wmix_ref, o_ref = rest
        x = x + _dot(a_ref[...], wmix_ref[...].astype(BF16))
    else:
        o_ref, = rest
    xg = (x * g_ref[...]).astype(BF16)
    r = lax.rsqrt(jnp.mean(x * x, axis=-1, keepdims=True) + NORM_EPS)
    y = jnp.zeros_like(x)
    for c in range(d_ff // FFN_CHUNK):
        lo = c * FFN_CHUNK
        gate = r * _dot(xg, win_ref[:, lo:lo + FFN_CHUNK].astype(BF16))
        up = r * _dot(xg, win_ref[:, d_ff + lo:d_ff + lo + FFN_CHUNK].astype(BF16))
        h = (gate * jax.nn.sigmoid(gate) * up).astype(BF16)
        y = y + _dot(h, wout_ref[lo:lo + FFN_CHUNK, :].astype(BF16))
    o_ref[...] = x + 0.5 * y


def _ffn(x, gain, w_in, w_out, mix=None):
    n = x.shape[0]
    d_ff = w_out.shape[0]
    assert d_ff % FFN_CHUNK == 0 and n % FFN_ROWS == 0
    row = pl.BlockSpec((FFN_ROWS, D_MODEL), lambda i: (i, 0))
    operands = [x, gain.reshape(1, D_MODEL), w_in, w_out]
    in_specs = [row, _resident((1, D_MODEL)), _resident(w_in.shape), _resident(w_out.shape)]
    resident = 4 * (w_in.size + w_out.size) + 4 * 4 * FFN_ROWS * D_MODEL \
        + 4 * FFN_ROWS * (3 * D_MODEL + 3 * FFN_CHUNK)
    if mix is not None:
        a, w_mix = mix
        operands += [a, w_mix]
        in_specs += [row, _resident(w_mix.shape)]
        resident += 4 * w_mix.size + 2 * 2 * FFN_ROWS * D_MODEL
    return pl.pallas_call(
        functools.partial(_ffn_kernel, d_ff=d_ff),
        grid=(n // FFN_ROWS,),
        in_specs=in_specs,
        out_specs=row,
        out_shape=jax.ShapeDtypeStruct((n, D_MODEL), F32),
        compiler_params=pltpu.CompilerParams(
            dimension_semantics=("arbitrary",), vmem_limit_bytes=_vmem_limit(resident)),
        name="ffn",
    )(*operands)


def _head_rms_norm(t, mean_ref, gain_row):
    outs = []
    for c in range(D_MODEL // V7X_MXU_DIM):
        tc = t[:, c * V7X_MXU_DIM:(c + 1) * V7X_MXU_DIM]
        hi, lo = _split_bf16(tc * tc)
        ms = _dot(hi, mean_ref[...]) + _dot(lo, mean_ref[...])
        outs.append(tc * lax.rsqrt(ms + NORM_EPS))
    return jnp.concatenate(outs, axis=-1) * gain_row


def _qkv_kernel(x_ref, g_ref, w_ref, mean_ref, qg_ref, kg_ref, q_ref, k_ref, v_ref):
    xn = _rms_norm_rows(x_ref[...], g_ref[...]).astype(BF16)
    q = _dot(xn, w_ref[:, 0:D_MODEL].astype(BF16))
    q_ref[...] = (_head_rms_norm(q, mean_ref, qg_ref[...]) * (1.0 / math.sqrt(SB_HEAD_DIM))).astype(BF16)
    k = _dot(xn, w_ref[:, D_MODEL:2 * D_MODEL].astype(BF16))
    k_ref[...] = _head_rms_norm(k, mean_ref, kg_ref[...]).astype(BF16)
    v_ref[...] = _dot(xn, w_ref[:, 2 * D_MODEL:3 * D_MODEL].astype(BF16)).astype(BF16)


def _head_mean_matrix():
    r = lax.broadcasted_iota(jnp.int32, (V7X_MXU_DIM, V7X_MXU_DIM), 0) // SB_HEAD_DIM
    c = lax.broadcasted_iota(jnp.int32, (V7X_MXU_DIM, V7X_MXU_DIM), 1) // SB_HEAD_DIM
    return jnp.where(r == c, 1.0 / SB_HEAD_DIM, 0.0).astype(BF16)


def _qkv(x, gain, w_qkv, q_gain, k_gain):
    n = x.shape[0]
    row = pl.BlockSpec((PROJ_ROWS, D_MODEL), lambda i: (i, 0))
    out = jax.ShapeDtypeStruct((n, D_MODEL), BF16)
    resident = 4 * w_qkv.size + 4 * 2 * PROJ_ROWS * D_MODEL + 3 * 2 * 2 * PROJ_ROWS * D_MODEL \
        + 4 * 4 * PROJ_ROWS * D_MODEL
    tile_gain = lambda g: jnp.tile(g, SB_HEADS).reshape(1, D_MODEL)
    return pl.pallas_call(
        _qkv_kernel,
        grid=(n // PROJ_ROWS,),
        in_specs=[row, _resident((1, D_MODEL)), _resident(w_qkv.shape),
                  _resident((V7X_MXU_DIM, V7X_MXU_DIM)),
                  _resident((1, D_MODEL)), _resident((1, D_MODEL))],
        out_specs=[row, row, row],
        out_shape=[out, out, out],
        compiler_params=pltpu.CompilerParams(
            dimension_semantics=("arbitrary",), vmem_limit_bytes=_vmem_limit(resident)),
        name="qkv",
    )(x, gain.reshape(1, D_MODEL), w_qkv, _head_mean_matrix(), tile_gain(q_gain), tile_gain(k_gain))


SweepState = collections.namedtuple("SweepState", "x total carry acc")


def _sb_kernel(q_ref, k_ref, v_ref, cs_ref, o_ref, kh_s, vh_s, x_s, total_s, carry_s, acc_s, *, seq):
    Q, K = SB_QUERIES, SB_KEYS
    lanes = [slice(g * V7X_LANES, (g + 1) * V7X_LANES) for g in range(SB_GROUPS)]

    lane = lax.broadcasted_iota(jnp.int32, (K, V7X_LANES), 1)
    head_lanes = [jnp.where((lane < SB_HEAD_DIM) == (h == 0), 1.0, 0.0).astype(BF16) for h in range(2)]

    def head_rows(kb, count=1):
        return pl.ds(pl.multiple_of(kb * 2 * K, 2 * K), count * 2 * K)

    def split_heads(kb, _):
        rows = pl.ds(pl.multiple_of(kb * K, K), K)
        for g in range(SB_GROUPS):
            for src, dst in ((k_ref, kh_s), (v_ref, vh_s)):
                blk = src[rows, lanes[g]]
                dst[g, head_rows(kb), :] = jnp.concatenate(
                    [blk * head_lanes[0], blk * head_lanes[1]], axis=0)
        return 0

    lax.fori_loop(0, seq // K, split_heads, 0)

    def score(st, qs, kbs, diagonal):
        work = [(g, i, kb) for g in range(SB_GROUPS) for i, kb in enumerate(kbs)]
        first_row = lambda i: max(SB_STEP - 1 - i, 0) * K if diagonal else 0
        logits, s_parts = [], []
        for g, i, kb in work:
            r0 = first_row(i)
            z = lax.dot_general(qs[g][r0:], kh_s[g, head_rows(kb), :], (((1,), (1,)), ((), ())),
                                preferred_element_type=F32)
            s = jnp.maximum(z, 0.0) + jnp.log(1.0 + jnp.exp2(jnp.abs(z) * -LOG2_E))
            if diagonal and i < SB_STEP:
                row = lax.broadcasted_iota(jnp.int32, (Q - r0, 2 * K), 0)
                col = lax.broadcasted_iota(jnp.int32, (Q - r0, 2 * K), 1)
                earlier = (col & (K - 1)) < row
                s = jnp.where(earlier, s, 0.0)
                z = jnp.where(earlier, z, SB_MASKED_LOGIT)
            s_hi, s_lo = _split_bf16(s)
            for h in range(2):
                s_parts.append(jnp.concatenate([s_hi[:, h * K:(h + 1) * K],
                                                s_lo[:, h * K:(h + 1) * K]], axis=1))
            logits.append(z)
        sums = _dot(jnp.concatenate(s_parts, axis=0), cs_ref[...])
        offset = 0
        for n, (g, i, kb) in enumerate(work):
            r0 = first_row(i)
            h0 = sums[offset:offset + Q - r0]
            h1 = sums[offset + Q - r0:offset + 2 * (Q - r0)]
            offset += 2 * (Q - r0)
            from_key = jnp.concatenate([h0[:, :K], h1[:, :K]], axis=1)
            total = jnp.concatenate([h0[:, K:], h1[:, K:]], axis=1)
            st.x[g, i, r0:, :] = logits[n] - from_key if i == 0 else logits[n] - from_key - later[r0:]
            if r0:
                st.x[g, i, :r0, :] = jnp.full((r0, 2 * K), SB_MASKED_LOGIT, F32)
                total = jnp.concatenate([jnp.zeros((r0, 2 * K), F32), total], axis=0)
            later = total if i == 0 else later + total
            if i == len(kbs) - 1:
                st.total[g] = later

    def fold(st, kbs):
        for g in range(SB_GROUPS):
            carry = st.carry[g]
            w = [jnp.exp(st.x[g, i] - carry).astype(BF16) for i in reversed(range(len(kbs)))]
            values = vh_s[g, head_rows(kbs[-1], len(kbs)), :]
            st.acc[g] += _dot(jnp.concatenate(w, axis=1), values)
            st.carry[g] = carry + st.total[g]

    def smallest_carry(st):
        smallest = None
        for g in range(SB_GROUPS):
            carry = st.carry[g]
            smallest = carry if smallest is None else jnp.minimum(smallest, carry)
        smallest = jnp.minimum(smallest[:, :K], smallest[:, K:])
        smallest = jnp.min(smallest.reshape(Q // V7X_SUBLANES, V7X_SUBLANES, K), axis=0)
        return jnp.min(smallest, axis=0, keepdims=True)[0, 0]

    def group(qi, j):
        top = (qi + 1) * SB_STEP - 1
        return [top - j * SB_STEP - n for n in range(SB_STEP)]

    def q_blocks(qis):
        slots = [SweepState(x_s.at[n], total_s.at[n], carry_s.at[n], acc_s.at[n])
                 for n in range(len(qis))]
        rows = [pl.ds(pl.multiple_of(qi * Q, Q), Q) for qi in qis]
        queries = [[q_ref[r, lanes[g]] for g in range(SB_GROUPS)] for r in rows]
        firsts = [group(qi, 0) if isinstance(qi, int) and qi == 0 else group(qi, 0) + group(qi, 1)
                  for qi in qis]
        for st, qs, first in zip(slots, queries, firsts):
            st.carry[...] = jnp.zeros_like(st.carry)
            st.acc[...] = jnp.zeros_like(st.acc)
            score(st, qs, first, True)
        for st, first in zip(slots, firsts):
            fold(st, first)
        alive = [smallest_carry(st) for st in slots]
        for st, qi, qs, r, first, smallest in zip(slots, qis, queries, rows, firsts, alive):
            if len(first) > SB_STEP:
                def sweep(state, st=st, qi=qi, qs=qs):
                    j = state[0] + 1
                    score(st, qs, group(qi, j), False)
                    fold(st, group(qi, j))
                    return j, smallest_carry(st)

                lax.while_loop(lambda state, qi=qi: (state[0] < qi) & (state[1] <= SB_DEAD_CARRY),
                               sweep, (jnp.int32(1), smallest))
            for g in range(SB_GROUPS):
                o_ref[r, lanes[g]] = st.acc[g].astype(o_ref.dtype)

    n_q = seq // Q
    sets = n_q // SB_TOGETHER
    q_blocks(list(range(min(SB_TOGETHER, n_q))))
    lax.fori_loop(1, sets,
                  lambda m, _: q_blocks([m * SB_TOGETHER + n for n in range(SB_TOGETHER)]) or 0, 0)
    if n_q > SB_TOGETHER and n_q % SB_TOGETHER:
        q_blocks(list(range(sets * SB_TOGETHER, n_q)))


def _sb_constants():
    r = lax.broadcasted_iota(jnp.int32, (2 * SB_KEYS, 2 * SB_KEYS), 0) % SB_KEYS
    c = lax.broadcasted_iota(jnp.int32, (2 * SB_KEYS, 2 * SB_KEYS), 1)
    return jnp.where((c >= SB_KEYS) | (r >= c), 1.0, 0.0).astype(BF16)


def _stick_breaking(q, k, v, *, batch, seq):
    assert seq % SB_QUERIES == 0 and SB_QUERIES == SB_STEP * SB_KEYS and 2 * SB_KEYS == V7X_MXU_DIM
    cs = _sb_constants()
    width = SB_GROUPS * V7X_LANES
    assert D_MODEL % width == 0
    group = pl.BlockSpec((None, seq, width), lambda b, g: (b, 0, g))
    per_head = pltpu.VMEM((SB_GROUPS, 2 * seq, V7X_LANES), BF16)
    scores = (SB_GROUPS, SB_QUERIES, 2 * SB_KEYS)
    resident = 2 * 4 * 2 * seq * width + 2 * cs.size + 2 * 2 * 2 * seq * width \
        + 4 * 8 * SB_TOGETHER * SB_GROUPS * SB_STEP * SB_QUERIES * 2 * SB_KEYS
    return pl.pallas_call(
        functools.partial(_sb_kernel, seq=seq),
        grid=(batch, D_MODEL // width),
        in_specs=[group, group, group, _resident(cs.shape)],
        out_specs=group,
        out_shape=jax.ShapeDtypeStruct((batch, seq, D_MODEL), BF16),
        scratch_shapes=[per_head, per_head,
                        pltpu.VMEM((SB_TOGETHER, SB_GROUPS, 2 * SB_STEP) + scores[1:], F32),
                        pltpu.VMEM((SB_TOGETHER,) + scores, F32),
                        pltpu.VMEM((SB_TOGETHER,) + scores, F32),
                        pltpu.VMEM((SB_TOGETHER, SB_GROUPS, SB_QUERIES, V7X_LANES), F32)],
        compiler_params=pltpu.CompilerParams(
            dimension_semantics=("arbitrary", "arbitrary"),
            vmem_limit_bytes=_vmem_limit(resident)),
        name="stick_breaking",
    )(q, k, v, cs)


def _rglru_kernel(x_ref, g_ref, win_ref, cw_ref, cb_ref, wg_ref, br_ref, bi_ref, lam_ref, wo_ref,
                  o_ref, xt_s, ot_s, xb_s, y_s, ri_s, a_s, u_s, h_s, state_s, win_s, wo_s, *, batch):
    rows = LRU_STEPS * batch
    halo = CONV_W * batch
    lane_tiles = D_MODEL // V7X_LANES

    @pl.when(pl.program_id(0) == 0)
    def _():
        xb_s[:, 0:halo, :] = jnp.zeros((xb_s.shape[0], halo, V7X_MXU_DIM), F32)
        state_s[...] = jnp.zeros_like(state_s)
        win_s[...] = win_ref[...].astype(BF16)
        wo_s[...] = wo_ref[...].astype(BF16)

    for b in range(batch):
        for l in range(lane_tiles):
            xt_s[l, pl.ds(b, LRU_STEPS, stride=batch), :] = \
                x_ref[b, :, l * V7X_LANES:(l + 1) * V7X_LANES]
    x = jnp.concatenate([xt_s[l] for l in range(lane_tiles)], axis=1)
    xn = _rms_norm_rows(x, g_ref[...]).astype(BF16)
    lam = lam_ref[...]
    log_sig_lam = jnp.minimum(lam, 0.0) - jnp.log1p(jnp.exp(-jnp.abs(lam)))

    for g in range(D_MODEL // V7X_MXU_DIM):
        cols = slice(g * V7X_MXU_DIM, (g + 1) * V7X_MXU_DIM)
        xb_s[g, halo:halo + rows, :] = _dot(xn, win_s[:, cols])
        y_s[g] = _dot(xn, win_s[:, D_MODEL + cols.start:D_MODEL + cols.stop])
        xc = cb_ref[:, cols] + cw_ref[CONV_W - 1:CONV_W, cols] * xb_s[g, halo:halo + rows, :]
        for j in range(CONV_W - 1):
            off = halo - (CONV_W - 1 - j) * batch
            xc = xc + cw_ref[j:j + 1, cols] * xb_s[g, off:off + rows, :]
        u_s[g] = xc
        ri_s[g] = _dot(xc.astype(BF16), wg_ref[g])
        r = jax.nn.sigmoid(ri_s[g, :, :V7X_MXU_DIM] + br_ref[:, cols])
        i = jax.nn.sigmoid(ri_s[g, :, V7X_MXU_DIM:] + bi_ref[:, cols])
        log_a = (LRU_C * r) * log_sig_lam[:, cols]
        a_s[g] = jnp.exp(log_a)
        th = jnp.tanh(log_a)
        u_s[g] = jnp.sqrt(-2.0 * th / (1.0 - th)) * (i * u_s[g])
    xb_s[:, 0:halo, :] = xb_s[:, rows:rows + halo, :]

    def tiles(ref, sl):
        return jnp.concatenate([ref[g, sl, :] for g in range(ref.shape[0])], axis=1)

    def step(t, h):
        sl = pl.ds(pl.multiple_of(t * batch, batch), batch)
        h = tiles(a_s, sl) * h + tiles(u_s, sl)
        h_s[sl, :] = h
        return h

    state_s[...] = lax.fori_loop(0, LRU_STEPS, step, state_s[...], unroll=8)
    y = jax.nn.gelu(tiles(y_s, slice(None)), approximate=True)
    out = x + _dot((h_s[...] * y).astype(BF16), wo_s[...])
    for l in range(lane_tiles):
        ot_s[l] = out[:, l * V7X_LANES:(l + 1) * V7X_LANES]
    for b in range(batch):
        for l in range(lane_tiles):
            o_ref[b, :, l * V7X_LANES:(l + 1) * V7X_LANES] = \
                ot_s[l, pl.ds(b, LRU_STEPS, stride=batch), :]


def _block_diag_tiles(w):
    per = V7X_MXU_DIM // LRU_BLOCK_W
    w = w.reshape(LRU_BLOCKS // per, per, LRU_BLOCK_W, LRU_BLOCK_W)
    eye = jnp.eye(per, dtype=w.dtype)
    return jnp.einsum("gpcd,pq->gpcqd", w, eye).reshape(LRU_BLOCKS // per, V7X_MXU_DIM, V7X_MXU_DIM)


def _rglru(x, gain, w_in, conv_w, conv_b, w_r, b_r, w_i, b_i, lam, w_o):
    batch, seq, _ = x.shape
    assert batch == V7X_SUBLANES and seq % LRU_STEPS == 0
    rows = LRU_STEPS * batch
    w_gate = jnp.concatenate([_block_diag_tiles(w_r), _block_diag_tiles(w_i)], axis=-1).astype(BF16)
    vec = lambda p: p.reshape(1, D_MODEL)
    chunk = pl.BlockSpec((batch, LRU_STEPS, D_MODEL), lambda i: (0, i, 0))
    slabs = pltpu.VMEM((D_MODEL // V7X_LANES, rows, V7X_LANES), F32)
    tile_major = lambda r: pltpu.VMEM((D_MODEL // V7X_MXU_DIM, r, V7X_MXU_DIM), F32)
    resident = 6 * (w_in.size + w_o.size) + 2 * w_gate.size + 4 * 4 * rows * D_MODEL \
        + 4 * 9 * (rows + CONV_W * batch) * D_MODEL + 4 * 6 * rows * D_MODEL
    return pl.pallas_call(
        functools.partial(_rglru_kernel, batch=batch),
        grid=(seq // LRU_STEPS,),
        in_specs=[chunk, _resident((1, D_MODEL)), _resident(w_in.shape),
                  _resident((CONV_W, D_MODEL)), _resident((1, D_MODEL)),
                  _resident(w_gate.shape), _resident((1, D_MODEL)), _resident((1, D_MODEL)),
                  _resident((1, D_MODEL)), _resident(w_o.shape)],
        out_specs=chunk,
        out_shape=jax.ShapeDtypeStruct(x.shape, F32),
        scratch_shapes=[slabs, slabs,
                        tile_major(rows + CONV_W * batch),
                        tile_major(rows),
                        pltpu.VMEM((D_MODEL // V7X_MXU_DIM, rows, 2 * V7X_MXU_DIM), F32),
                        tile_major(rows), tile_major(rows),
                        pltpu.VMEM((rows, D_MODEL), F32),
                        pltpu.VMEM((batch, D_MODEL), F32),
                        pltpu.VMEM(w_in.shape, BF16), pltpu.VMEM(w_o.shape, BF16)],
        compiler_params=pltpu.CompilerParams(
            dimension_semantics=("arbitrary",), vmem_limit_bytes=_vmem_limit(resident)),
        name="rglru",
    )(x, vec(gain), w_in, conv_w, vec(conv_b), w_gate, vec(b_r), vec(b_i), vec(lam), w_o)


def kernel(x, l0_ff1_norm, l0_ff1_w_in, l0_ff1_w_out, l0_mix_norm, l0_sb_w_qkv, l0_sb_q_norm, l0_sb_k_norm, l0_sb_w_o, l0_ff2_norm, l0_ff2_w_in, l0_ff2_w_out, l1_ff1_norm, l1_ff1_w_in, l1_ff1_w_out, l1_mix_norm, l1_lru_w_in, l1_lru_conv_w, l1_lru_conv_b, l1_lru_w_r, l1_lru_b_r, l1_lru_w_i, l1_lru_b_i, l1_lru_lambda, l1_lru_w_o, l1_ff2_norm, l1_ff2_w_in, l1_ff2_w_out):
    batch, seq, d = x.shape
    assert d == D_MODEL
    n = batch * seq
    as_seq = lambda t: t.reshape(batch, seq, D_MODEL)
    x = x.reshape(n, D_MODEL)

    x = _ffn(x, l0_ff1_norm, l0_ff1_w_in, l0_ff1_w_out)
    q, k, v = _qkv(x, l0_mix_norm, l0_sb_w_qkv, l0_sb_q_norm, l0_sb_k_norm)
    o = _stick_breaking(as_seq(q), as_seq(k), as_seq(v), batch=batch, seq=seq)
    x = _ffn(x, l0_ff2_norm, l0_ff2_w_in, l0_ff2_w_out, mix=(o.reshape(n, D_MODEL), l0_sb_w_o))

    x = _ffn(x, l1_ff1_norm, l1_ff1_w_in, l1_ff1_w_out)
    x = _rglru(as_seq(x), l1_mix_norm, l1_lru_w_in, l1_lru_conv_w, l1_lru_conv_b,
               l1_lru_w_r, l1_lru_b_r, l1_lru_w_i, l1_lru_b_i, l1_lru_lambda, l1_lru_w_o)
    x = _ffn(x.reshape(n, D_MODEL), l1_ff2_norm, l1_ff2_w_in, l1_ff2_w_out)
    return as_seq(x)
```
